```python
import jax, jax.numpy as jnp
from jax import lax
import numpy as np

D_MODEL = 1024
BATCH = 4
SEQ = 8192
DEPTH = 4

CHUNK = 64
Q_BLOCK = 128
MEM_LEN = 256
FFN_HIDDEN = 2048
CONV_CH = 512
CONV_WIDTH = 31
SB_HEADS = 4
SB_HEAD_DIM = 128
MLA_HEADS = 4
MLA_NOPE = 128
MLA_ROPE = 64
MLA_V = 128
MLA_Q_LORA = 256
MLA_KV_LORA = 256
MEM_HEADS = 4
MEM_HEAD_DIM = 128
N_BRANCH = 4
BRANCH_WIDTH = 512
ROPE_BASE = 10000.0
EPS = 1e-6
NEG_INF = -1e30
IN_SPLITS = (2 * CONV_CH, 3 * SB_HEADS * SB_HEAD_DIM, MLA_Q_LORA, MLA_KV_LORA, MLA_ROPE, MEM_HEADS * MEM_HEAD_DIM, N_BRANCH * D_MODEL)
IN_WIDTH = 2 * CONV_CH + 3 * SB_HEADS * SB_HEAD_DIM + MLA_Q_LORA + MLA_KV_LORA + MLA_ROPE + MEM_HEADS * MEM_HEAD_DIM + N_BRANCH * D_MODEL

kernel_name = "hybrid_chunk_causal_encoder_trunk"


def rms_norm(x, g):
    xf = x.astype(jnp.float32)
    y = xf * lax.rsqrt(jnp.mean(xf * xf, axis=-1, keepdims=True) + EPS)
    return (y * g.astype(jnp.float32)).astype(x.dtype)


def layer_norm(x, g, b):
    xf = x.astype(jnp.float32)
    mu = jnp.mean(xf, axis=-1, keepdims=True)
    var = jnp.mean(jnp.square(xf - mu), axis=-1, keepdims=True)
    y = (xf - mu) * lax.rsqrt(var + EPS)
    return (y * g.astype(jnp.float32) + b.astype(jnp.float32)).astype(x.dtype)


def swiglu_ffn(h, w_in, w_out):
    gate, up = jnp.split(h @ w_in, 2, axis=-1)
    return (jax.nn.silu(gate) * up) @ w_out


def rope_tables(positions):
    inv_freq = ROPE_BASE ** (-jnp.arange(0, MLA_ROPE, 2, dtype=jnp.float32) / MLA_ROPE)
    ang = positions.astype(jnp.float32)[..., None] * inv_freq
    return jnp.cos(ang)[:, :, None, :], jnp.sin(ang)[:, :, None, :]


def rope_tail(x, cos, sin):
    x_pass, x_rot = x[..., :-MLA_ROPE], x[..., -MLA_ROPE:]
    x1, x2 = jnp.split(x_rot, 2, axis=-1)
    c, s = cos.astype(x.dtype), sin.astype(x.dtype)
    return jnp.concatenate([x_pass, x1 * c - x2 * s, x2 * c + x1 * s], axis=-1)


def causal_block_sweep(block_fn, q, k, v):
    nb = q.shape[1] // Q_BLOCK
    outs = []
    for i in range(nb):
        lo, hi = i * Q_BLOCK, (i + 1) * Q_BLOCK
        outs.append(block_fn(q[:, lo:hi], k[:, :hi], v[:, :hi], i))
    return jnp.concatenate(outs, axis=1)


def stick_breaking_attention(q, k, v):
    scale = SB_HEAD_DIM ** -0.5
    r = jnp.arange(Q_BLOCK, dtype=jnp.int32)
    later_in_block = (r[:, None] > r[None, :]).astype(jnp.float32)

    def block(qb, kk, vv, i):
        B, Q, H, _ = qb.shape
        L = kk.shape[1]
        nk = L // Q_BLOCK
        q_pos = i * Q_BLOCK + r
        strict = jnp.arange(L, dtype=jnp.int32)[None, :] < q_pos[:, None]
        z = jnp.einsum('bqhd,bkhd->bhqk', qb, kk).astype(jnp.float32) * scale
        log_keep = jnp.where(strict, -jax.nn.softplus(z), 0.0)
        lk = log_keep.reshape(B, H, Q, nk, Q_BLOCK)
        within = jnp.einsum('bhqnk,kj->bhqnj', lk, later_in_block)
        blk = jnp.sum(lk, axis=-1)
        later_blocks = lax.cumsum(blk, axis=3, reverse=True) - blk
        later = (within + later_blocks[..., None]).reshape(B, H, Q, L)
        w = jnp.exp(jnp.where(strict, z + log_keep + later, NEG_INF))
        return jnp.einsum('bhqk,bkhd->bqhd', w.astype(vv.dtype), vv)

    return causal_block_sweep(block, q, k, v)


def chunk_causal_softmax_attention(q, k, v, scale):
    r = jnp.arange(Q_BLOCK, dtype=jnp.int32)

    def block(qb, kk, vv, i):
        L = kk.shape[1]
        q_chunk = (i * Q_BLOCK + r) // CHUNK
        allowed = (jnp.arange(L, dtype=jnp.int32) // CHUNK)[None, :] <= q_chunk[:, None]
        s = jnp.einsum('bqhd,bkhd->bhqk', qb, kk).astype(jnp.float32) * scale
        p = jax.nn.softmax(jnp.where(allowed, s, NEG_INF), axis=-1)
        return jnp.einsum('bhqk,bkhd->bqhd', p.astype(vv.dtype), vv)

    return causal_block_sweep(block, q, k, v)


def conformer_conv_branch(u, dw, b, ln_g, ln_b):
    a, g = jnp.split(u, 2, axis=-1)
    y = a * jax.nn.sigmoid(g)
    y = lax.conv_general_dilated(
        y, dw[:, None, :].astype(y.dtype), window_strides=(1,),
        padding=[(CONV_WIDTH - 1, 0)], dimension_numbers=('NWC', 'WIO', 'NWC'),
        feature_group_count=CONV_CH) + b
    return jax.nn.silu(layer_norm(y, ln_g, ln_b))


def stick_breaking_branch(qkv, q_hnorm, k_hnorm):
    B, S = qkv.shape[:2]
    qkv = qkv.reshape(B, S, 3, SB_HEADS, SB_HEAD_DIM)
    q = rms_norm(qkv[:, :, 0], q_hnorm)
    k = rms_norm(qkv[:, :, 1], k_hnorm)
    o = stick_breaking_attention(q, k, qkv[:, :, 2])
    return o.reshape(B, S, SB_HEADS * SB_HEAD_DIM)


def mla_branch(q_lat, kv_lat, k_rope, cos, sin, q_norm, w_uq, kv_norm, w_ukv, q_hnorm, k_hnorm):
    B, S = q_lat.shape[:2]
    q = (rms_norm(q_lat, q_norm) @ w_uq).reshape(B, S, MLA_HEADS, MLA_NOPE + MLA_ROPE)
    kv = (rms_norm(kv_lat, kv_norm) @ w_ukv).reshape(B, S, MLA_HEADS, MLA_NOPE + MLA_V)
    k_nope, v = kv[..., :MLA_NOPE], kv[..., MLA_NOPE:]
    k_r = jnp.broadcast_to(k_rope[:, :, None, :], (B, S, MLA_HEADS, MLA_ROPE))
    k = jnp.concatenate([k_nope, k_r], axis=-1)
    q = rope_tail(rms_norm(q, q_hnorm), cos, sin)
    k = rope_tail(rms_norm(k, k_hnorm), cos, sin)
    o = chunk_causal_softmax_attention(q, k, v, (MLA_NOPE + MLA_ROPE) ** -0.5)
    return o.reshape(B, S, MLA_HEADS * MLA_V)


def memory_branch(q_raw, mem, mem_norm, w_kv, q_hnorm, k_hnorm):
    B, S = q_raw.shape[:2]
    M = mem.shape[1]
    q = rms_norm(q_raw.reshape(B, S, MEM_HEADS, MEM_HEAD_DIM), q_hnorm)
    kv = (rms_norm(mem, mem_norm) @ w_kv).reshape(B, M, 2, MEM_HEADS, MEM_HEAD_DIM)
    k = rms_norm(kv[:, :, 0], k_hnorm)
    v = kv[:, :, 1]
    s = jnp.einsum('bshd,bmhd->bhsm', q, k).astype(jnp.float32) * (MEM_HEAD_DIM ** -0.5)
    p = jax.nn.softmax(s, axis=-1).astype(v.dtype)
    o = jnp.einsum('bhsm,bmhd->bshd', p, v)
    return o.reshape(B, S, MEM_HEADS * MEM_HEAD_DIM)


def setup_inputs(seed: int = 0) -> dict:
    key = jax.random.key(seed)
    ks = jax.random.split(key, 32)
    L, D, F = DEPTH, D_MODEL, FFN_HIDDEN

    def w(k, shape, fan_in):
        return jax.random.normal(k, shape, jnp.float32) * (fan_in ** -0.5)

    def gain(k, shape):
        return 1.0 + 0.02 * jax.random.normal(k, shape, jnp.float32)

    offsets = jax.random.randint(ks[2], (BATCH, 1), 0, 4096, dtype=jnp.int32)
    positions = offsets + jnp.arange(SEQ, dtype=jnp.int32)[None, :]
    return {
        "x": jax.random.normal(ks[0], (BATCH, SEQ, D), jnp.float32),
        "mem": jax.random.normal(ks[1], (BATCH, MEM_LEN, D), jnp.float32),
        "positions": positions,
        "ffn1_norm": gain(ks[3], (L, D)),
        "ffn1_w_in": w(ks[4], (L, D, 2 * F), D),
        "ffn1_w_out": w(ks[5], (L, F, D), F),
        "mix_norm": gain(ks[6], (L, D)),
        "w_in": w(ks[7], (L, D, IN_WIDTH), D),
        "conv_dw": w(ks[8], (L, CONV_WIDTH, CONV_CH), CONV_WIDTH),
        "conv_b": 0.02 * jax.random.normal(ks[9], (L, CONV_CH), jnp.float32),
        "conv_ln_g": gain(ks[10], (L, CONV_CH)),
        "conv_ln_b": 0.02 * jax.random.normal(ks[11], (L, CONV_CH), jnp.float32),
        "sb_q_hnorm": gain(ks[12], (L, SB_HEAD_DIM)),
        "sb_k_hnorm": gain(ks[13], (L, SB_HEAD_DIM)),
        "mla_q_norm": gain(ks[14], (L, MLA_Q_LORA)),
        "mla_w_uq": w(ks[15], (L, MLA_Q_LORA, MLA_HEADS * (MLA_NOPE + MLA_ROPE)), MLA_Q_LORA),
        "mla_kv_norm": gain(ks[16], (L, MLA_KV_LORA)),
        "mla_w_ukv": w(ks[17], (L, MLA_KV_LORA, MLA_HEADS * (MLA_NOPE + MLA_V)), MLA_KV_LORA),
        "mla_q_hnorm": gain(ks[18], (L, MLA_NOPE + MLA_ROPE)),
        "mla_k_hnorm": gain(ks[19], (L, MLA_NOPE + MLA_ROPE)),
        "mem_norm": gain(ks[20], (L, D)),
        "mem_w_kv": w(ks[21], (L, D, 2 * MEM_HEADS * MEM_HEAD_DIM), D),
        "mem_q_hnorm": gain(ks[22], (L, MEM_HEAD_DIM)),
        "mem_k_hnorm": gain(ks[23], (L, MEM_HEAD_DIM)),
        "w_branch": w(ks[24], (L, N_BRANCH, BRANCH_WIDTH, D), BRANCH_WIDTH),
        "w_out": w(ks[25], (L, D, D), D),
        "ffn2_norm": gain(ks[26], (L, D)),
        "ffn2_w_in": w(ks[27], (L, D, 2 * F), D),
        "ffn2_w_out": w(ks[28], (L, F, D), F),
    }


def reference(x, mem, positions, ffn1_norm, ffn1_w_in, ffn1_w_out, mix_norm, w_in,
              conv_dw, conv_b, conv_ln_g, conv_ln_b, sb_q_hnorm, sb_k_hnorm,
              mla_q_norm, mla_w_uq, mla_kv_norm, mla_w_ukv, mla_q_hnorm, mla_k_hnorm,
              mem_norm, mem_w_kv, mem_q_hnorm, mem_k_hnorm, w_branch, w_out,
              ffn2_norm, ffn2_w_in, ffn2_w_out):
    B, S, D = x.shape
    cos, sin = rope_tables(positions)
    split_at = np.cumsum(IN_SPLITS)[:-1].tolist()
    for l in range(DEPTH):
        x = x + 0.5 * swiglu_ffn(rms_norm(x, ffn1_norm[l]), ffn1_w_in[l], ffn1_w_out[l])
        h = rms_norm(x, mix_norm[l])
        u = h @ w_in[l]
        conv_u, sb_qkv, q_lat, kv_lat, k_rope, mem_q, gate_logits = jnp.split(u, split_at, axis=-1)
        branches = (
            conformer_conv_branch(conv_u, conv_dw[l], conv_b[l], conv_ln_g[l], conv_ln_b[l]),
            stick_breaking_branch(sb_qkv, sb_q_hnorm[l], sb_k_hnorm[l]),
            mla_branch(q_lat, kv_lat, k_rope, cos, sin, mla_q_norm[l], mla_w_uq[l],
                       mla_kv_norm[l], mla_w_ukv[l], mla_q_hnorm[l], mla_k_hnorm[l]),
            memory_branch(mem_q, mem, mem_norm[l], mem_w_kv[l], mem_q_hnorm[l], mem_k_hnorm[l]),
        )
        gates = jax.nn.sigmoid(gate_logits).reshape(B, S, N_BRANCH, D)
        merged = gates[:, :, 0, :] * (branches[0] @ w_branch[l, 0])
        for i in range(1, N_BRANCH):
            merged = merged + gates[:, :, i, :] * (branches[i] @ w_branch[l, i])
        x = x + merged @ w_out[l]
        x = x + 0.5 * swiglu_ffn(rms_norm(x, ffn2_norm[l]), ffn2_w_in[l], ffn2_w_out[l])
    return x
```

```python
import functools

import jax
import jax.numpy as jnp
from jax import lax
from jax.experimental import pallas as pl
from jax.experimental.pallas import tpu as pltpu

F32 = jnp.float32
BF16 = jnp.bfloat16

D_MODEL = 1024
FFN_HIDDEN = 2048
CHUNK = 64
CONV_CH = 512
CONV_WIDTH = 31
HEADS = 4
HEAD_DIM = 128
MLA_NOPE = 128
MLA_ROPE = 64
MLA_QK_PAD = 256
MLA_Q_LORA = 256
MLA_KV_LORA = 256
N_BRANCH = 4
BRANCH_WIDTH = 512
ROPE_BASE = 10000.0
EPS = 1e-6
MASK_VALUE = -1e30

LANES = 128
CONV_HALO = 32
V7X_VMEM_LIMIT_BYTES = 56 * 1024 * 1024

_C_CONV = 0
_C_SBQ = 1024
_C_SBK = 1536
_C_SBV = 2048
_C_QLAT = 2560
_C_KVLAT = 2816
_C_KR = 3072
_C_KRSW = 3200
_C_MEMQ = 3328
_C_END = 3840


def _params(*sem):
    return pltpu.CompilerParams(dimension_semantics=sem, vmem_limit_bytes=V7X_VMEM_LIMIT_BYTES)


def _resident(shape):
    nd = len(shape)
    return pl.BlockSpec(shape, lambda *_: (0,) * nd)


def _rms_scale(x):
    return lax.rsqrt(jnp.mean(x * x, axis=-1, keepdims=True) + EPS)


FFN_ROWS = 512
FFN_COLS = 512


def _ffn_kernel(x_ref, g_ref, win_ref, wout_ref, o_ref, act_ref):
    x = x_ref[...]
    h = (x * _rms_scale(x) * g_ref[...]).astype(BF16)
    for c in range(FFN_HIDDEN // FFN_COLS):
        lo = c * FFN_COLS
        gate = jnp.dot(h, win_ref[:, lo:lo + FFN_COLS], preferred_element_type=F32)
        up = jnp.dot(h, win_ref[:, FFN_HIDDEN + lo:FFN_HIDDEN + lo + FFN_COLS], preferred_element_type=F32)
        act_ref[:, lo:lo + FFN_COLS] = (gate * jax.nn.sigmoid(gate) * up).astype(BF16)
    y = jnp.dot(act_ref[...], wout_ref[...], preferred_element_type=F32)
    o_ref[...] = x + 0.5 * y


def _ffn(x2d, g, w_in, w_out):
    n, d = x2d.shape
    tm = min(FFN_ROWS, n)
    return pl.pallas_call(
        _ffn_kernel,
        grid=(n // tm,),
        in_specs=[
            pl.BlockSpec((tm, d), lambda i: (i, 0)),
            _resident(g.shape),
            _resident(w_in.shape),
            _resident(w_out.shape),
        ],
        out_specs=pl.BlockSpec((tm, d), lambda i: (i, 0)),
        out_shape=jax.ShapeDtypeStruct((n, d), F32),
        scratch_shapes=[pltpu.VMEM((tm, FFN_HIDDEN), BF16)],
        compiler_params=_params("parallel"),
        name="ffn_half_step",
    )(x2d, g, w_in, w_out)


PROJ_ROWS = 256


def _head_sumsq(x):
    return jnp.sum(x * x, axis=-1, keepdims=True)


def _proj_kernel(x_ref, mixg_ref, w_ref, cos_ref, sin_ref, qn_ref, kvn_ref, wuq_ref, wukv_ref, gains_ref,
                 y_ref, sbq_ref, sbk_ref, sbv_ref, mq_ref, mk_ref, mv_ref, memq_ref):
    x = x_ref[...]
    h = (x * _rms_scale(x) * mixg_ref[...]).astype(BF16)
    u = jnp.dot(h, w_ref[...], preferred_element_type=F32)
    gains = gains_ref[...]
    sb_qg, sb_kg, mem_qg = gains[0:1], gains[1:2], gains[2:3]
    q_ga, q_gb, q_gbsw = gains[3:4], gains[4:5], gains[5:6]
    k_ga, k_gb, k_gbsw = gains[6:7], gains[7:8], gains[8:9]
    cos_t = cos_ref[...]
    sin_t = sin_ref[...]

    y_ref[...] = (u[:, _C_CONV:_C_CONV + CONV_CH]
                  * jax.nn.sigmoid(u[:, _C_CONV + CONV_CH:_C_CONV + 2 * CONV_CH])).astype(BF16)

    for hd in range(HEADS):
        lo = hd * HEAD_DIM
        q = u[:, _C_SBQ + lo:_C_SBQ + lo + HEAD_DIM]
        k = u[:, _C_SBK + lo:_C_SBK + lo + HEAD_DIM]
        m = u[:, _C_MEMQ + lo:_C_MEMQ + lo + HEAD_DIM]
        sbq_ref[:, lo:lo + HEAD_DIM] = (q * lax.rsqrt(_head_sumsq(q) / HEAD_DIM + EPS) * sb_qg).astype(BF16)
        sbk_ref[:, lo:lo + HEAD_DIM] = (k * lax.rsqrt(_head_sumsq(k) / HEAD_DIM + EPS) * sb_kg).astype(BF16)
        memq_ref[:, lo:lo + HEAD_DIM] = (m * lax.rsqrt(_head_sumsq(m) / HEAD_DIM + EPS) * mem_qg).astype(BF16)
    sbv_ref[...] = u[:, _C_SBV:_C_SBV + HEADS * HEAD_DIM].astype(BF16)

    q_lat = u[:, _C_QLAT:_C_QLAT + MLA_Q_LORA]
    kv_lat = u[:, _C_KVLAT:_C_KVLAT + MLA_KV_LORA]
    q_lat_n = (q_lat * _rms_scale(q_lat) * qn_ref[...]).astype(BF16)
    kv_lat_n = (kv_lat * _rms_scale(kv_lat) * kvn_ref[...]).astype(BF16)
    q_up = jnp.dot(q_lat_n, wuq_ref[...], preferred_element_type=F32)
    kv_up = jnp.dot(kv_lat_n, wukv_ref[...], preferred_element_type=F32)
    kr = u[:, _C_KR:_C_KR + LANES]
    kr_sw = u[:, _C_KRSW:_C_KRSW + LANES]
    kr_ss = _head_sumsq(kr)
    qk_dim = MLA_NOPE + MLA_ROPE
    for hd in range(HEADS):
        qa = q_up[:, hd * 384:hd * 384 + LANES]
        qb = q_up[:, hd * 384 + LANES:hd * 384 + 2 * LANES]
        qbsw = q_up[:, hd * 384 + 2 * LANES:hd * 384 + 3 * LANES]
        qr = lax.rsqrt((_head_sumsq(qa) + _head_sumsq(qb)) / qk_dim + EPS)
        mq_ref[:, hd * MLA_QK_PAD:hd * MLA_QK_PAD + LANES] = (qa * qr * q_ga).astype(BF16)
        mq_ref[:, hd * MLA_QK_PAD + LANES:(hd + 1) * MLA_QK_PAD] = (
            (qb * q_gb * cos_t + qbsw * q_gbsw * sin_t) * qr).astype(BF16)
        ka = kv_up[:, hd * 256:hd * 256 + LANES]
        kr_inv = lax.rsqrt((_head_sumsq(ka) + kr_ss) / qk_dim + EPS)
        mk_ref[:, hd * MLA_QK_PAD:hd * MLA_QK_PAD + LANES] = (ka * kr_inv * k_ga).astype(BF16)
        mk_ref[:, hd * MLA_QK_PAD + LANES:(hd + 1) * MLA_QK_PAD] = (
            (kr * k_gb * cos_t + kr_sw * k_gbsw * sin_t) * kr_inv).astype(BF16)
        mv_ref[:, hd * HEAD_DIM:(hd + 1) * HEAD_DIM] = kv_up[:, hd * 256 + LANES:(hd + 1) * 256].astype(BF16)


def _proj(x2d, mix_g, w_main, cos_t, sin_t, q_norm, kv_norm, w_uq, w_ukv, gains):
    n, d = x2d.shape
    tm = min(PROJ_ROWS, n)
    row = lambda w: pl.BlockSpec((tm, w), lambda i: (i, 0))
    widths = (CONV_CH, 512, 512, 512, HEADS * MLA_QK_PAD, HEADS * MLA_QK_PAD, 512, 512)
    return pl.pallas_call(
        _proj_kernel,
        grid=(n // tm,),
        in_specs=[row(d), _resident(mix_g.shape), _resident(w_main.shape), row(LANES), row(LANES),
                  _resident(q_norm.shape), _resident(kv_norm.shape), _resident(w_uq.shape),
                  _resident(w_ukv.shape), _resident(gains.shape)],
        out_specs=[row(w) for w in widths],
        out_shape=[jax.ShapeDtypeStruct((n, w), BF16) for w in widths],
        compiler_params=_params("parallel"),
        name="mix_in_proj",
    )(x2d, mix_g, w_main, cos_t, sin_t, q_norm, kv_norm, w_uq, w_ukv, gains)


CONV_ROWS = 256
CONV_CHUNK = 64


def _conv_kernel(y_ref, halo_ref, dw_ref, b_ref, lng_ref, lnb_ref, o_ref, buf_ref):
    j = pl.program_id(1)
    tm = y_ref.shape[1]
    halo = halo_ref[0].astype(F32)
    buf_ref[0:CONV_HALO, :] = jnp.where(j > 0, halo, jnp.zeros_like(halo))
    buf_ref[CONV_HALO:, :] = y_ref[0].astype(F32)
    dw = dw_ref[...]
    first = CONV_HALO - (CONV_WIDTH - 1)
    for c in range(tm // CONV_CHUNK):
        r0 = c * CONV_CHUNK
        acc = jnp.zeros((CONV_CHUNK, CONV_CH), F32) + b_ref[...]
        for k in range(CONV_WIDTH):
            acc = acc + dw[k:k + 1, :] * buf_ref[r0 + first + k:r0 + first + k + CONV_CHUNK, :]
        mu = jnp.mean(acc, axis=-1, keepdims=True)
        cen = acc - mu
        var = jnp.mean(cen * cen, axis=-1, keepdims=True)
        z = cen * lax.rsqrt(var + EPS) * lng_ref[...] + lnb_ref[...]
        o_ref[0, r0:r0 + CONV_CHUNK, :] = (z * jax.nn.sigmoid(z)).astype(BF16)


def _conv_branch(y, dw, b, ln_g, ln_b):
    bsz, s, ch = y.shape
    tm = min(CONV_ROWS, s)
    ratio = tm // CONV_HALO
    return pl.pallas_call(
        _conv_kernel,
        grid=(bsz, s // tm),
        in_specs=[
            pl.BlockSpec((1, tm, ch), lambda b_, j: (b_, j, 0)),
            pl.BlockSpec((1, CONV_HALO, ch), lambda b_, j: (b_, jnp.maximum(j * ratio - 1, 0), 0)),
            _resident(dw.shape), _resident(b.shape), _resident(ln_g.shape), _resident(ln_b.shape),
        ],
        out_specs=pl.BlockSpec((1, tm, ch), lambda b_, j: (b_, j, 0)),
        out_shape=jax.ShapeDtypeStruct((bsz, s, ch), BF16),
        scratch_shapes=[pltpu.VMEM((tm + CONV_HALO, ch), F32)],
        compiler_params=_params("parallel", "parallel"),
        name="conformer_conv",
    )(y, y, dw, b, ln_g, ln_b)


ATT_BLOCK = 256


def _tn_dot(a, b):
    return lax.dot_general(a, b, (((0,), (0,)), ((), ())), preferred_element_type=F32)


def _sb_block(k_blk, v_blk, q_t, ut, carry, out_t, strict):
    z = jnp.dot(k_blk, q_t, preferred_element_type=F32)
    log_keep = -(jnp.maximum(z, 0.0) + jnp.log(1.0 + jnp.exp(-jnp.abs(z))))
    if strict is not None:
        log_keep = jnp.where(strict, log_keep, 0.0)
    later = jnp.dot(ut, log_keep.astype(BF16), preferred_element_type=F32) + carry
    logw = z + log_keep + later
    if strict is not None:
        logw = jnp.where(strict, logw, MASK_VALUE)
    w = jnp.exp(logw)
    out_t = out_t + _tn_dot(v_blk, w.astype(BF16))
    carry = carry + jnp.sum(log_keep, axis=0, keepdims=True)
    return carry, out_t


def _sb_kernel(q_ref, k_ref, v_ref, o_ref):
    i = pl.program_id(2)
    blk = q_ref.shape[1]
    q_t = q_ref[0].astype(F32).T.astype(BF16)
    key = lax.broadcasted_iota(jnp.int32, (blk, blk), 0)
    col = lax.broadcasted_iota(jnp.int32, (blk, blk), 1)
    strict = key < col
    ut = (col > key).astype(BF16)

    def block(kb, carry, out_t, mask):
        start = pl.multiple_of(kb * blk, blk)
        return _sb_block(k_ref[0, pl.ds(start, blk), :], v_ref[0, pl.ds(start, blk), :], q_t, ut, carry, out_t, mask)

    carry, out_t = block(i, jnp.zeros((1, blk), F32), jnp.zeros((HEAD_DIM, blk), F32), strict)

    def body(r, state):
        return block(i - 1 - r, state[0], state[1], None)

    carry, out_t = lax.fori_loop(0, i, body, (carry, out_t))
    o_ref[0] = out_t.T.astype(BF16)


def _sb_attention(q, k, v):
    bsz, s, _ = q.shape
    blk = min(ATT_BLOCK, s)
    return pl.pallas_call(
        _sb_kernel,
        grid=(bsz, HEADS, s // blk),
        in_specs=[
            pl.BlockSpec((1, blk, HEAD_DIM), lambda b, h, i: (b, i, h)),
            pl.BlockSpec((1, s, HEAD_DIM), lambda b, h, i: (b, 0, h)),
            pl.BlockSpec((1, s, HEAD_DIM), lambda b, h, i: (b, 0, h)),
        ],
        out_specs=pl.BlockSpec((1, blk, HEAD_DIM), lambda b, h, i: (b, i, h)),
        out_shape=jax.ShapeDtypeStruct((bsz, s, HEADS * HEAD_DIM), BF16),
        compiler_params=_params("parallel", "parallel", "arbitrary"),
        name="stick_breaking_attention",
    )(q, k, v)


def _softmax_block(k_blk, v_blk, q_t, m, l, out_t, allowed):
    s = jnp.dot(k_blk, q_t, preferred_element_type=F32)
    if allowed is not None:
        s = jnp.where(allowed, s, MASK_VALUE)
    m_new = jnp.maximum(m, jnp.max(s, axis=0, keepdims=True))
    alpha = jnp.exp(m - m_new)
    p = jnp.exp(s - m_new)
    l = alpha * l + jnp.sum(p, axis=0, keepdims=True)
    out_t = alpha * out_t + _tn_dot(v_blk, p.astype(BF16))
    return m_new, l, out_t


def _mla_kernel(q_ref, k_ref, v_ref, o_ref):
    i = pl.program_id(2)
    blk = q_ref.shape[1]
    q_t = q_ref[0].astype(F32).T.astype(BF16)
    key = lax.broadcasted_iota(jnp.int32, (blk, blk), 0)
    col = lax.broadcasted_iota(jnp.int32, (blk, blk), 1)
    allowed = (key // CHUNK) <= (col // CHUNK)

    def block(kb, m, l, out_t, mask):
        start = pl.multiple_of(kb * blk, blk)
        return _softmax_block(k_ref[0, pl.ds(start, blk), :], v_ref[0, pl.ds(start, blk), :], q_t, m, l, out_t, mask)

    m0 = jnp.full((1, blk), MASK_VALUE, F32)
    state = block(i, m0, jnp.zeros((1, blk), F32), jnp.zeros((HEAD_DIM, blk), F32), allowed)

    def body(r, st):
        return block(r, st[0], st[1], st[2], None)

    _, l, out_t = lax.fori_loop(0, i, body, state)
    o_ref[0] = (out_t / l).T.astype(BF16)


def _mla_attention(q, k, v):
    bsz, s, _ = q.shape
    blk = min(ATT_BLOCK, s)
    return pl.pallas_call(
        _mla_kernel,
        grid=(bsz, HEADS, s // blk),
        in_specs=[
            pl.BlockSpec((1, blk, MLA_QK_PAD), lambda b, h, i: (b, i, h)),
            pl.BlockSpec((1, s, MLA_QK_PAD), lambda b, h, i: (b, 0, h)),
            pl.BlockSpec((1, s, HEAD_DIM), lambda b, h, i: (b, 0, h)),
        ],
        out_specs=pl.BlockSpec((1, blk, HEAD_DIM), lambda b, h, i: (b, i, h)),
        out_shape=jax.ShapeDtypeStruct((bsz, s, HEADS * HEAD_DIM), BF16),
        compiler_params=_params("parallel", "parallel", "arbitrary"),
        name="chunk_causal_latent_attention",
    )(q, k, v)


def _memkv_kernel(mem_ref, g_ref, w_ref, kg_ref, k_ref, v_ref):
    m = mem_ref[0]
    mn = (m * _rms_scale(m) * g_ref[...]).astype(BF16)
    kv = jnp.dot(mn, w_ref[...], preferred_element_type=F32)
    for hd in range(HEADS):
        lo = hd * HEAD_DIM
        k = kv[:, lo:lo + HEAD_DIM]
        k_ref[0, :, lo:lo + HEAD_DIM] = (k * lax.rsqrt(_head_sumsq(k) / HEAD_DIM + EPS) * kg_ref[...]).astype(BF16)
    v_ref[0] = kv[:, HEADS * HEAD_DIM:].astype(BF16)


def _memkv(mem, g, w_kv, k_gain):
    bsz, m, d = mem.shape
    width = HEADS * HEAD_DIM
    blk = lambda w: pl.BlockSpec((1, m, w), lambda b: (b, 0, 0))
    return pl.pallas_call(
        _memkv_kernel,
        grid=(bsz,),
        in_specs=[blk(d), _resident(g.shape), _resident(w_kv.shape), _resident(k_gain.shape)],
        out_specs=[blk(width), blk(width)],
        out_shape=[jax.ShapeDtypeStruct((bsz, m, width), BF16)] * 2,
        compiler_params=_params("parallel"),
        name="memory_kv",
    )(mem, g, w_kv, k_gain)


MERGE_ROWS = 256


def _merge_kernel(x_ref, mixg_ref, wg_ref, wb_ref, wo_ref, conv_ref, sb_ref, mla_ref, memq_ref, memk_ref, memv_ref,
                  o_ref, memo_ref):
    x = x_ref[0]
    h = (x * _rms_scale(x) * mixg_ref[...]).astype(BF16)

    for hd in range(HEADS):
        lo = hd * HEAD_DIM
        s = lax.dot_general(memq_ref[0, :, lo:lo + HEAD_DIM], memk_ref[0, :, lo:lo + HEAD_DIM],
                            (((1,), (1,)), ((), ())), preferred_element_type=F32)
        p = jnp.exp(s - jnp.max(s, axis=-1, keepdims=True))
        o = jnp.dot(p.astype(BF16), memv_ref[0, :, lo:lo + HEAD_DIM], preferred_element_type=F32)
        memo_ref[:, lo:lo + HEAD_DIM] = (o / jnp.sum(p, axis=-1, keepdims=True)).astype(BF16)

    branches = (conv_ref[0], sb_ref[0], mla_ref[0], memo_ref[...])
    merged = None
    for i, br in enumerate(branches):
        gate = jax.nn.sigmoid(jnp.dot(h, wg_ref[:, i * D_MODEL:(i + 1) * D_MODEL], preferred_element_type=F32))
        term = gate * jnp.dot(br, wb_ref[i], preferred_element_type=F32)
        merged = term if merged is None else merged + term
    o_ref[0] = x + jnp.dot(merged.astype(BF16), wo_ref[...], preferred_element_type=F32)


def _merge(x, mix_g, w_gate, w_branch, w_out, conv_o, sb_o, mla_o, mem_q, mem_k, mem_v):
    bsz, s, d = x.shape
    tm = min(MERGE_ROWS, s)
    row = lambda w: pl.BlockSpec((1, tm, w), lambda b, j: (b, j, 0))
    mem = pl.BlockSpec((1,) + mem_k.shape[1:], lambda b, j: (b, 0, 0))
    return pl.pallas_call(
        _merge_kernel,
        grid=(bsz, s // tm),
        in_specs=[row(d), _resident(mix_g.shape), _resident(w_gate.shape), _resident(w_branch.shape),
                  _resident(w_out.shape), row(BRANCH_WIDTH), row(BRANCH_WIDTH), row(BRANCH_WIDTH),
                  row(BRANCH_WIDTH), mem, mem],
        out_specs=row(d),
        out_shape=jax.ShapeDtypeStruct((bsz, s, d), F32),
        scratch_shapes=[pltpu.VMEM((tm, BRANCH_WIDTH), BF16)],
        compiler_params=_params("parallel", "parallel"),
        name="gated_merge_out_proj",
    )(x, mix_g, w_gate, w_branch, w_out, conv_o, sb_o, mla_o, mem_q, mem_k, mem_v)


def _pad_lanes(a, width=LANES):
    return jnp.pad(a, [(0, 0)] * (a.ndim - 1) + [(0, width - a.shape[-1])])


def _swap_halves(a):
    half = a.shape[-1] // 2
    return jnp.concatenate([a[..., half:], a[..., :half]], axis=-1)


def _rope_tables(positions):
    inv_freq = ROPE_BASE ** (-jnp.arange(0, MLA_ROPE, 2, dtype=F32) / MLA_ROPE)
    ang = positions.astype(F32)[..., None] * inv_freq
    cos, sin = jnp.cos(ang), jnp.sin(ang)
    cos_t = _pad_lanes(jnp.concatenate([cos, cos], axis=-1))
    sin_t = _pad_lanes(jnp.concatenate([-sin, sin], axis=-1))
    return cos_t, sin_t


def _layer_params(l, w_in, sb_q_hnorm, sb_k_hnorm, mla_w_uq, mla_q_hnorm, mla_k_hnorm, mem_q_hnorm):
    w = w_in[l]
    kr = w[:, 3072:3136]
    w_main = jnp.concatenate(
        [w[:, :3072], _pad_lanes(kr), _pad_lanes(_swap_halves(kr)), w[:, 3136:3648]], axis=1).astype(BF16)
    w_gate = w[:, 3648:].astype(BF16)

    wuq = mla_w_uq[l].reshape(MLA_Q_LORA, HEADS, MLA_NOPE + MLA_ROPE)
    rot = wuq[..., MLA_NOPE:]
    w_uq = jnp.concatenate([wuq[..., :MLA_NOPE], _pad_lanes(rot), _pad_lanes(_swap_halves(rot))], axis=-1)
    w_uq = w_uq.reshape(MLA_Q_LORA, HEADS * 3 * LANES).astype(BF16)

    sb_scale = HEAD_DIM ** -0.5
    mla_scale = (MLA_NOPE + MLA_ROPE) ** -0.5
    qg, kg = mla_q_hnorm[l] * mla_scale, mla_k_hnorm[l]
    rows = [sb_q_hnorm[l] * sb_scale, sb_k_hnorm[l], mem_q_hnorm[l] * sb_scale,
            qg[:MLA_NOPE], _pad_lanes(qg[MLA_NOPE:]), _pad_lanes(_swap_halves(qg[MLA_NOPE:])),
            kg[:MLA_NOPE], _pad_lanes(kg[MLA_NOPE:]), _pad_lanes(_swap_halves(kg[MLA_NOPE:]))]
    gains = jnp.stack(rows + [jnp.zeros((LANES,), F32)] * (16 - len(rows)))
    return w_main, w_gate, w_uq, gains


def kernel(x, mem, positions, ffn1_norm, ffn1_w_in, ffn1_w_out, mix_norm, w_in, conv_dw, conv_b, conv_ln_g,
           conv_ln_b, sb_q_hnorm, sb_k_hnorm, mla_q_norm, mla_w_uq, mla_kv_norm, mla_w_ukv, mla_q_hnorm,
           mla_k_hnorm, mem_norm, mem_w_kv, mem_q_hnorm, mem_k_hnorm, w_branch, w_out, ffn2_norm, ffn2_w_in,
           ffn2_w_out):
    bsz, s, d = x.shape
    depth = w_in.shape[0]
    n = bsz * s
    cos_t, sin_t = _rope_tables(positions)
    cos_t, sin_t = cos_t.reshape(n, LANES), sin_t.reshape(n, LANES)
    row = lambda a: a.reshape(1, -1)

    for l in range(depth):
        w_main, w_gate, w_uq, gains = _layer_params(
            l, w_in, sb_q_hnorm, sb_k_hnorm, mla_w_uq, mla_q_hnorm, mla_k_hnorm, mem_q_hnorm)

        x2d = _ffn(x.reshape(n, d), row(ffn1_norm[l]), ffn1_w_in[l].astype(BF16), ffn1_w_out[l].astype(BF16))

        y, sbq, sbk, sbv, mq, mk, mv, memq = _proj(
            x2d, row(mix_norm[l]), w_main, cos_t, sin_t, row(mla_q_norm[l]), row(mla_kv_norm[l]), w_uq,
            mla_w_ukv[l].astype(BF16), gains)
        seq = lambda a: a.reshape(bsz, s, a.shape[-1])

        conv_o = _conv_branch(seq(y), conv_dw[l], row(conv_b[l]), row(conv_ln_g[l]), row(conv_ln_b[l]))
        sb_o = _sb_attention(seq(sbq), seq(sbk), seq(sbv))
        mla_o = _mla_attention(seq(mq), seq(mk), seq(mv))
        mem_k, mem_v = _memkv(mem, row(mem_norm[l]), mem_w_kv[l].astype(BF16), row(mem_k_hnorm[l]))

        x = _merge(seq(x2d), row(mix_norm[l]), w_gate, w_branch[l].astype(BF16), w_out[l].astype(BF16),
                   conv_o, sb_o, mla_o, seq(memq), mem_k, mem_v)

        x = _ffn(x.reshape(n, d), row(ffn2_norm[l]), ffn2_w_in[l].astype(BF16),
                 ffn2_w_out[l].astype(BF16)).reshape(bsz, s, d)
    return x
```

```python
import functools

import jax
import jax.numpy as jnp
from jax import lax
from jax.experimental import pallas as pl
from jax.experimental.pallas import tpu as pltpu

F32 = jnp.float32
BF16 = jnp.bfloat16

D_MODEL = 1024
FFN_HIDDEN = 2048
CHUNK = 64
CONV_CH = 512
CONV_WIDTH = 31
HEADS = 4
HEAD_DIM = 128
MLA_NOPE = 128
MLA_ROPE = 64
MLA_QK_PAD = 256
MLA_Q_LORA = 256
MLA_KV_LORA = 256
N_BRANCH = 4
BRANCH_WIDTH = 512
ROPE_BASE = 10000.0
EPS = 1e-6
MASK_VALUE = -1e30
LOG2_E = 1.4426950408889634
SB_UNDERFLOW_LOG = -110.0

LANES = 128
CONV_HALO = 32
V7X_VMEM_LIMIT_BYTES = 56 * 1024 * 1024

_C_CONV = 0
_C_SBQ = 1024
_C_SBK = 1536
_C_SBV = 2048
_C_QLAT = 2560
_C_KVLAT = 2816
_C_KR = 3072
_C_KRSW = 3200
_C_MEMQ = 3328
_C_END = 3840


def _params(*sem):
    return pltpu.CompilerParams(dimension_semantics=sem, vmem_limit_bytes=V7X_VMEM_LIMIT_BYTES)


def _resident(shape):
    nd = len(shape)
    return pl.BlockSpec(shape, lambda *_: (0,) * nd)


def _rms_scale(x):
    return lax.rsqrt(jnp.mean(x * x, axis=-1, keepdims=True) + EPS)


FFN_ROWS = 512
FFN_COLS = 512


def _ffn_kernel(x_ref, g_ref, win_ref, wout_ref, o_ref, act_ref):
    x = x_ref[...]
    h = (x * _rms_scale(x) * g_ref[...]).astype(BF16)
    for c in range(FFN_HIDDEN // FFN_COLS):
        lo = c * FFN_COLS
        gate = jnp.dot(h, win_ref[:, lo:lo + FFN_COLS], preferred_element_type=F32)
        up = jnp.dot(h, win_ref[:, FFN_HIDDEN + lo:FFN_HIDDEN + lo + FFN_COLS], preferred_element_type=F32)
        act_ref[:, lo:lo + FFN_COLS] = (gate * jax.nn.sigmoid(gate) * up).astype(BF16)
    y = jnp.dot(act_ref[...], wout_ref[...], preferred_element_type=F32)
    o_ref[...] = x + 0.5 * y


def _ffn(x2d, g, w_in, w_out):
    n, d = x2d.shape
    tm = min(FFN_ROWS, n)
    return pl.pallas_call(
        _ffn_kernel,
        grid=(n // tm,),
        in_specs=[
            pl.BlockSpec((tm, d), lambda i: (i, 0)),
            _resident(g.shape),
            _resident(w_in.shape),
            _resident(w_out.shape),
        ],
        out_specs=pl.BlockSpec((tm, d), lambda i: (i, 0)),
        out_shape=jax.ShapeDtypeStruct((n, d), F32),
        scratch_shapes=[pltpu.VMEM((tm, FFN_HIDDEN), BF16)],
        compiler_params=_params("parallel"),
        name="ffn_half_step",
    )(x2d, g, w_in, w_out)


PROJ_ROWS = 256


def _head_sumsq(x):
    return jnp.sum(x * x, axis=-1, keepdims=True)


def _proj_kernel(x_ref, mixg_ref, w_ref, cos_ref, sin_ref, qn_ref, kvn_ref, wuq_ref, wukv_ref, gains_ref,
                 y_ref, sbq_ref, sbk_ref, sbv_ref, mq_ref, mk_ref, mv_ref, memq_ref):
    x = x_ref[...]
    h = (x * _rms_scale(x) * mixg_ref[...]).astype(BF16)
    u = jnp.dot(h, w_ref[...], preferred_element_type=F32)
    gains = gains_ref[...]
    sb_qg, sb_kg, mem_qg = gains[0:1], gains[1:2], gains[2:3]
    q_ga, q_gb, q_gbsw = gains[3:4], gains[4:5], gains[5:6]
    k_ga, k_gb, k_gbsw = gains[6:7], gains[7:8], gains[8:9]
    cos_t = cos_ref[...]
    sin_t = sin_ref[...]

    y_ref[...] = (u[:, _C_CONV:_C_CONV + CONV_CH]
                  * jax.nn.sigmoid(u[:, _C_CONV + CONV_CH:_C_CONV + 2 * CONV_CH])).astype(BF16)

    for hd in range(HEADS):
        lo = hd * HEAD_DIM
        q = u[:, _C_SBQ + lo:_C_SBQ + lo + HEAD_DIM]
        k = u[:, _C_SBK + lo:_C_SBK + lo + HEAD_DIM]
        m = u[:, _C_MEMQ + lo:_C_MEMQ + lo + HEAD_DIM]
        sbq_ref[:, lo:lo + HEAD_DIM] = (q * lax.rsqrt(_head_sumsq(q) / HEAD_DIM + EPS) * sb_qg).astype(BF16)
        sbk_ref[:, lo:lo + HEAD_DIM] = (k * lax.rsqrt(_head_sumsq(k) / HEAD_DIM + EPS) * sb_kg).astype(BF16)
        memq_ref[:, lo:lo + HEAD_DIM] = (m * lax.rsqrt(_head_sumsq(m) / HEAD_DIM + EPS) * mem_qg).astype(BF16)
    sbv_ref[...] = u[:, _C_SBV:_C_SBV + HEADS * HEAD_DIM].astype(BF16)

    q_lat = u[:, _C_QLAT:_C_QLAT + MLA_Q_LORA]
    kv_lat = u[:, _C_KVLAT:_C_KVLAT + MLA_KV_LORA]
    q_lat_n = (q_lat * _rms_scale(q_lat) * qn_ref[...]).astype(BF16)
    kv_lat_n = (kv_lat * _rms_scale(kv_lat) * kvn_ref[...]).astype(BF16)
    q_up = jnp.dot(q_lat_n, wuq_ref[...], preferred_element_type=F32)
    kv_up = jnp.dot(kv_lat_n, wukv_ref[...], preferred_element_type=F32)
    kr = u[:, _C_KR:_C_KR + LANES]
    kr_sw = u[:, _C_KRSW:_C_KRSW + LANES]
    kr_ss = _head_sumsq(kr)
    qk_dim = MLA_NOPE + MLA_ROPE
    for hd in range(HEADS):
        qa = q_up[:, hd * 384:hd * 384 + LANES]
        qb = q_up[:, hd * 384 + LANES:hd * 384 + 2 * LANES]
        qbsw = q_up[:, hd * 384 + 2 * LANES:hd * 384 + 3 * LANES]
        qr = lax.rsqrt((_head_sumsq(qa) + _head_sumsq(qb)) / qk_dim + EPS)
        mq_ref[:, hd * MLA_QK_PAD:hd * MLA_QK_PAD + LANES] = (qa * qr * q_ga).astype(BF16)
        mq_ref[:, hd * MLA_QK_PAD + LANES:(hd + 1) * MLA_QK_PAD] = (
            (qb * q_gb * cos_t + qbsw * q_gbsw * sin_t) * qr).astype(BF16)
        ka = kv_up[:, hd * 256:hd * 256 + LANES]
        kr_inv = lax.rsqrt((_head_sumsq(ka) + kr_ss) / qk_dim + EPS)
        mk_ref[:, hd * MLA_QK_PAD:hd * MLA_QK_PAD + LANES] = (ka * kr_inv * k_ga).astype(BF16)
        mk_ref[:, hd * MLA_QK_PAD + LANES:(hd + 1) * MLA_QK_PAD] = (
            (kr * k_gb * cos_t + kr_sw * k_gbsw * sin_t) * kr_inv).astype(BF16)
        mv_ref[:, hd * HEAD_DIM:(hd + 1) * HEAD_DIM] = kv_up[:, hd * 256 + LANES:(hd + 1) * 256].astype(BF16)


def _proj(x2d, mix_g, w_main, cos_t, sin_t, q_norm, kv_norm, w_uq, w_ukv, gains):
    n, d = x2d.shape
    tm = min(PROJ_ROWS, n)
    row = lambda w: pl.BlockSpec((tm, w), lambda i: (i, 0))
    widths = (CONV_CH, 512, 512, 512, HEADS * MLA_QK_PAD, HEADS * MLA_QK_PAD, 512, 512)
    return pl.pallas_call(
        _proj_kernel,
        grid=(n // tm,),
        in_specs=[row(d), _resident(mix_g.shape), _resident(w_main.shape), row(LANES), row(LANES),
                  _resident(q_norm.shape), _resident(kv_norm.shape), _resident(w_uq.shape),
                  _resident(w_ukv.shape), _resident(gains.shape)],
        out_specs=[row(w) for w in widths],
        out_shape=[jax.ShapeDtypeStruct((n, w), BF16) for w in widths],
        compiler_params=_params("parallel"),
        name="mix_in_proj",
    )(x2d, mix_g, w_main, cos_t, sin_t, q_norm, kv_norm, w_uq, w_ukv, gains)


CONV_ROWS = 256
CONV_CHUNK = 64


def _conv_kernel(y_ref, halo_ref, dw_ref, b_ref, lng_ref, lnb_ref, o_ref, buf_ref):
    j = pl.program_id(1)
    tm = y_ref.shape[1]
    halo = halo_ref[0].astype(F32)
    buf_ref[0:CONV_HALO, :] = jnp.where(j > 0, halo, jnp.zeros_like(halo))
    buf_ref[CONV_HALO:, :] = y_ref[0].astype(F32)
    dw = dw_ref[...]
    first = CONV_HALO - (CONV_WIDTH - 1)
    for c in range(tm // CONV_CHUNK):
        r0 = c * CONV_CHUNK
        acc = jnp.zeros((CONV_CHUNK, CONV_CH), F32) + b_ref[...]
        for k in range(CONV_WIDTH):
            acc = acc + dw[k:k + 1, :] * buf_ref[r0 + first + k:r0 + first + k + CONV_CHUNK, :]
        mu = jnp.mean(acc, axis=-1, keepdims=True)
        cen = acc - mu
        var = jnp.mean(cen * cen, axis=-1, keepdims=True)
        z = cen * lax.rsqrt(var + EPS) * lng_ref[...] + lnb_ref[...]
        o_ref[0, r0:r0 + CONV_CHUNK, :] = (z * jax.nn.sigmoid(z)).astype(BF16)


def _conv_branch(y, dw, b, ln_g, ln_b):
    bsz, s, ch = y.shape
    tm = min(CONV_ROWS, s)
    ratio = tm // CONV_HALO
    return pl.pallas_call(
        _conv_kernel,
        grid=(bsz, s // tm),
        in_specs=[
            pl.BlockSpec((1, tm, ch), lambda b_, j: (b_, j, 0)),
            pl.BlockSpec((1, CONV_HALO, ch), lambda b_, j: (b_, jnp.maximum(j * ratio - 1, 0), 0)),
            _resident(dw.shape), _resident(b.shape), _resident(ln_g.shape), _resident(ln_b.shape),
        ],
        out_specs=pl.BlockSpec((1, tm, ch), lambda b_, j: (b_, j, 0)),
        out_shape=jax.ShapeDtypeStruct((bsz, s, ch), BF16),
        scratch_shapes=[pltpu.VMEM((tm + CONV_HALO, ch), F32)],
        compiler_params=_params("parallel", "parallel"),
        name="conformer_conv",
    )(y, y, dw, b, ln_g, ln_b)


ATT_BLOCK = 256


def _tn_dot(a, b):
    return lax.dot_general(a, b, (((0,), (0,)), ((), ())), preferred_element_type=F32)


def _sb_block(k_blk, v_blk, q_t, ut, carry, out_t, strict):
    z = jnp.dot(k_blk, q_t, preferred_element_type=F32)
    log_keep = -(jnp.maximum(z, 0.0) + jnp.log(1.0 + jnp.exp(-jnp.abs(z))))
    if strict is not None:
        log_keep = jnp.where(strict, log_keep, 0.0)
    later = jnp.dot(ut, log_keep.astype(BF16), preferred_element_type=F32) + carry
    logw = z + log_keep + later
    if strict is not None:
        logw = jnp.where(strict, logw, MASK_VALUE)
    w = jnp.exp(logw)
    out_t = out_t + _tn_dot(v_blk, w.astype(BF16))
    carry = carry + jnp.sum(log_keep, axis=0, keepdims=True)
    return carry, out_t


def _sb_kernel(q_ref, k_ref, v_ref, o_ref):
    i = pl.program_id(2)
    blk = q_ref.shape[1]
    q_t = q_ref[0].astype(F32).T.astype(BF16)
    key = lax.broadcasted_iota(jnp.int32, (blk, blk), 0)
    col = lax.broadcasted_iota(jnp.int32, (blk, blk), 1)
    strict = key < col
    ut = (col > key).astype(BF16)

    def block(kb, carry, out_t, mask):
        start = pl.multiple_of(kb * blk, blk)
        return _sb_block(k_ref[0, pl.ds(start, blk), :], v_ref[0, pl.ds(start, blk), :], q_t, ut, carry, out_t, mask)

    carry, out_t = block(i, jnp.zeros((1, blk), F32), jnp.zeros((HEAD_DIM, blk), F32), strict)

    def more(state):
        r, _, _, top = state
        return jnp.logical_and(r < i, top > SB_UNDERFLOW_LOG)

    def body(state):
        r, carry, out_t, _ = state
        carry, out_t = block(i - 1 - r, carry, out_t, None)
        return r + 1, carry, out_t, jnp.max(carry)

    _, _, out_t, _ = lax.while_loop(more, body, (jnp.int32(0), carry, out_t, jnp.max(carry)))
    o_ref[0] = out_t.T.astype(BF16)


def _sb_attention(q, k, v):
    bsz, s, _ = q.shape
    blk = min(ATT_BLOCK, s)
    return pl.pallas_call(
        _sb_kernel,
        grid=(bsz, HEADS, s // blk),
        in_specs=[
            pl.BlockSpec((1, blk, HEAD_DIM), lambda b, h, i: (b, i, h)),
            pl.BlockSpec((1, s, HEAD_DIM), lambda b, h, i: (b, 0, h)),
            pl.BlockSpec((1, s, HEAD_DIM), lambda b, h, i: (b, 0, h)),
        ],
        out_specs=pl.BlockSpec((1, blk, HEAD_DIM), lambda b, h, i: (b, i, h)),
        out_shape=jax.ShapeDtypeStruct((bsz, s, HEADS * HEAD_DIM), BF16),
        compiler_params=_params("parallel", "parallel", "arbitrary"),
        name="stick_breaking_attention",
    )(q, k, v)


MLA_Q_TILE = 512
MLA_K_TILE = 256


def _mla_kernel(q_ref, k_ref, v_ref, o_ref, qt_ref, sa_ref, sb_ref, m_ref, l_ref, acc_ref):
    i = pl.program_id(2)
    tq, tk = q_ref.shape[1], sa_ref.shape[0]
    per_q = tq // tk
    qt_ref[...] = q_ref[0].astype(F32).T.astype(BF16)
    m_ref[...] = jnp.full(m_ref.shape, MASK_VALUE, F32)
    l_ref[...] = jnp.zeros(l_ref.shape, F32)
    acc_ref[...] = jnp.zeros(acc_ref.shape, F32)

    def scores(kb, dst_ref):
        start = pl.multiple_of(kb * tk, tk)
        dst_ref[...] = jnp.dot(k_ref[0, pl.ds(start, tk), :], qt_ref[...], preferred_element_type=F32)

    def update(src_ref, kb, allowed):
        start = pl.multiple_of(kb * tk, tk)
        s = src_ref[...]
        if allowed is not None:
            s = jnp.where(allowed, s, MASK_VALUE)
        m_prev = m_ref[...]
        m_new = jnp.maximum(m_prev, jnp.max(s, axis=0, keepdims=True))
        alpha = jnp.exp2(m_prev - m_new)
        p = jnp.exp2(s - m_new)
        l_ref[...] = alpha * l_ref[...] + jnp.sum(p, axis=0, keepdims=True)
        acc_ref[...] = alpha * acc_ref[...] + _tn_dot(v_ref[0, pl.ds(start, tk), :], p.astype(BF16))
        m_ref[...] = m_new

    sa_ref[...] = jnp.dot(k_ref[0, 0:tk, :], qt_ref[...], preferred_element_type=F32)

    def pair(p, carry):
        scores(2 * p + 1, sb_ref)
        update(sa_ref, 2 * p, None)
        scores(2 * p + 2, sa_ref)
        update(sb_ref, 2 * p + 1, None)
        return carry

    lax.fori_loop(0, i * (per_q // 2), pair, 0)

    key = lax.broadcasted_iota(jnp.int32, (tk, tq), 0)
    col_chunk = lax.broadcasted_iota(jnp.int32, (tk, tq), 1) // CHUNK
    first = i * per_q
    for d in range(per_q):
        if d + 1 < per_q:
            scores(first + d + 1, sb_ref if d % 2 == 0 else sa_ref)
        update(sa_ref if d % 2 == 0 else sb_ref, first + d, (key + d * tk) // CHUNK <= col_chunk)
    o_ref[0] = (acc_ref[...] / l_ref[...]).T.astype(BF16)


def _mla_attention(q, k, v):
    bsz, s, _ = q.shape
    tq = min(MLA_Q_TILE, s)
    tk = min(MLA_K_TILE, tq // 2)
    assert tq == 2 * tk and s % tq == 0
    return pl.pallas_call(
        _mla_kernel,
        grid=(bsz, HEADS, s // tq),
        in_specs=[
            pl.BlockSpec((1, tq, MLA_QK_PAD), lambda b, h, i: (b, i, h)),
            pl.BlockSpec((1, s, MLA_QK_PAD), lambda b, h, i: (b, 0, h)),
            pl.BlockSpec((1, s, HEAD_DIM), lambda b, h, i: (b, 0, h)),
        ],
        out_specs=pl.BlockSpec((1, tq, HEAD_DIM), lambda b, h, i: (b, i, h)),
        out_shape=jax.ShapeDtypeStruct((bsz, s, HEADS * HEAD_DIM), BF16),
        scratch_shapes=[
            pltpu.VMEM((MLA_QK_PAD, tq), BF16),
            pltpu.VMEM((tk, tq), F32),
            pltpu.VMEM((tk, tq), F32),
            pltpu.VMEM((1, tq), F32),
            pltpu.VMEM((1, tq), F32),
            pltpu.VMEM((HEAD_DIM, tq), F32),
        ],
        compiler_params=_params("parallel", "parallel", "arbitrary"),
        name="chunk_causal_latent_attention",
    )(q, k, v)


def _memkv_kernel(mem_ref, g_ref, w_ref, kg_ref, k_ref, v_ref):
    m = mem_ref[0]
    mn = (m * _rms_scale(m) * g_ref[...]).astype(BF16)
    kv = jnp.dot(mn, w_ref[...], preferred_element_type=F32)
    for hd in range(HEADS):
        lo = hd * HEAD_DIM
        k = kv[:, lo:lo + HEAD_DIM]
        k_ref[0, :, lo:lo + HEAD_DIM] = (k * lax.rsqrt(_head_sumsq(k) / HEAD_DIM + EPS) * kg_ref[...]).astype(BF16)
    v_ref[0] = kv[:, HEADS * HEAD_DIM:].astype(BF16)


def _memkv(mem, g, w_kv, k_gain):
    bsz, m, d = mem.shape
    width = HEADS * HEAD_DIM
    blk = lambda w: pl.BlockSpec((1, m, w), lambda b: (b, 0, 0))
    return pl.pallas_call(
        _memkv_kernel,
        grid=(bsz,),
        in_specs=[blk(d), _resident(g.shape), _resident(w_kv.shape), _resident(k_gain.shape)],
        out_specs=[blk(width), blk(width)],
        out_shape=[jax.ShapeDtypeStruct((bsz, m, width), BF16)] * 2,
        compiler_params=_params("parallel"),
        name="memory_kv",
    )(mem, g, w_kv, k_gain)


MERGE_ROWS = 256


def _merge_kernel(x_ref, mixg_ref, wg_ref, wb_ref, wo_ref, conv_ref, sb_ref, mla_ref, memq_ref, memk_ref, memv_ref,
                  o_ref, memo_ref):
    x = x_ref[0]
    h = (x * _rms_scale(x) * mixg_ref[...]).astype(BF16)

    for hd in range(HEADS):
        lo = hd * HEAD_DIM
        s = lax.dot_general(memq_ref[0, :, lo:lo + HEAD_DIM], memk_ref[0, :, lo:lo + HEAD_DIM],
                            (((1,), (1,)), ((), ())), preferred_element_type=F32)
        p = jnp.exp(s - jnp.max(s, axis=-1, keepdims=True))
        o = jnp.dot(p.astype(BF16), memv_ref[0, :, lo:lo + HEAD_DIM], preferred_element_type=F32)
        memo_ref[:, lo:lo + HEAD_DIM] = (o / jnp.sum(p, axis=-1, keepdims=True)).astype(BF16)

    branches = (conv_ref[0], sb_ref[0], mla_ref[0], memo_ref[...])
    merged = None
    for i, br in enumerate(branches):
        gate = jax.nn.sigmoid(jnp.dot(h, wg_ref[:, i * D_MODEL:(i + 1) * D_MODEL], preferred_element_type=F32))
        term = gate * jnp.dot(br, wb_ref[i], preferred_element_type=F32)
        merged = term if merged is None else merged + term
    o_ref[0] = x + jnp.dot(merged.astype(BF16), wo_ref[...], preferred_element_type=F32)


def _merge(x, mix_g, w_gate, w_branch, w_out, conv_o, sb_o, mla_o, mem_q, mem_k, mem_v):
    bsz, s, d = x.shape
    tm = min(MERGE_ROWS, s)
    row = lambda w: pl.BlockSpec((1, tm, w), lambda b, j: (b, j, 0))
    mem = pl.BlockSpec((1,) + mem_k.shape[1:], lambda b, j: (b, 0, 0))
    return pl.pallas_call(
        _merge_kernel,
        grid=(bsz, s // tm),
        in_specs=[row(d), _resident(mix_g.shape), _resident(w_gate.shape), _resident(w_branch.shape),
                  _resident(w_out.shape), row(BRANCH_WIDTH), row(BRANCH_WIDTH), row(BRANCH_WIDTH),
                  row(BRANCH_WIDTH), mem, mem],
        out_specs=row(d),
        out_shape=jax.ShapeDtypeStruct((bsz, s, d), F32),
        scratch_shapes=[pltpu.VMEM((tm, BRANCH_WIDTH), BF16)],
        compiler_params=_params("parallel", "parallel"),
        name="gated_merge_out_proj",
    )(x, mix_g, w_gate, w_branch, w_out, conv_o, sb_o, mla_o, mem_q, mem_k, mem_v)


def _pad_lanes(a, width=LANES):
    return jnp.pad(a, [(0, 0)] * (a.ndim - 1) + [(0, width - a.shape[-1])])


def _swap_halves(a):
    half = a.shape[-1] // 2
    return jnp.concatenate([a[..., half:], a[..., :half]], axis=-1)


def _rope_tables(positions):
    inv_freq = ROPE_BASE ** (-jnp.arange(0, MLA_ROPE, 2, dtype=F32) / MLA_ROPE)
    ang = positions.astype(F32)[..., None] * inv_freq
    cos, sin = jnp.cos(ang), jnp.sin(ang)
    cos_t = _pad_lanes(jnp.concatenate([cos, cos], axis=-1))
    sin_t = _pad_lanes(jnp.concatenate([-sin, sin], axis=-1))
    return cos_t, sin_t


def _layer_params(l, w_in, sb_q_hnorm, sb_k_hnorm, mla_w_uq, mla_q_hnorm, mla_k_hnorm, mem_q_hnorm):
    w = w_in[l]
    kr = w[:, 3072:3136]
    w_main = jnp.concatenate(
        [w[:, :3072], _pad_lanes(kr), _pad_lanes(_swap_halves(kr)), w[:, 3136:3648]], axis=1).astype(BF16)
    w_gate = w[:, 3648:].astype(BF16)

    wuq = mla_w_uq[l].reshape(MLA_Q_LORA, HEADS, MLA_NOPE + MLA_ROPE)
    rot = wuq[..., MLA_NOPE:]
    w_uq = jnp.concatenate([wuq[..., :MLA_NOPE], _pad_lanes(rot), _pad_lanes(_swap_halves(rot))], axis=-1)
    w_uq = w_uq.reshape(MLA_Q_LORA, HEADS * 3 * LANES).astype(BF16)

    sb_scale = HEAD_DIM ** -0.5
    mla_scale = (MLA_NOPE + MLA_ROPE) ** -0.5 * LOG2_E
    qg, kg = mla_q_hnorm[l] * mla_scale, mla_k_hnorm[l]
    rows = [sb_q_hnorm[l] * sb_scale, sb_k_hnorm[l], mem_q_hnorm[l] * sb_scale,
            qg[:MLA_NOPE], _pad_lanes(qg[MLA_NOPE:]), _pad_lanes(_swap_halves(qg[MLA_NOPE:])),
            kg[:MLA_NOPE], _pad_lanes(kg[MLA_NOPE:]), _pad_lanes(_swap_halves(kg[MLA_NOPE:]))]
    gains = jnp.stack(rows + [jnp.zeros((LANES,), F32)] * (16 - len(rows)))
    return w_main, w_gate, w_uq, gains


def kernel(x, mem, positions, ffn1_norm, ffn1_w_in, ffn1_w_out, mix_norm, w_in, conv_dw, conv_b, conv_ln_g,
           conv_ln_b, sb_q_hnorm, sb_k_hnorm, mla_q_norm, mla_w_uq, mla_kv_norm, mla_w_ukv, mla_q_hnorm,
           mla_k_hnorm, mem_norm, mem_w_kv, mem_q_hnorm, mem_k_hnorm, w_branch, w_out, ffn2_norm, ffn2_w_in,
           ffn2_w_out):
    bsz, s, d = x.shape
    depth = w_in.shape[0]
    n = bsz * s
    cos_t, sin_t = _rope_tables(positions)
    cos_t, sin_t = cos_t.reshape(n, LANES), sin_t.reshape(n, LANES)
    row = lambda a: a.reshape(1, -1)

    for l in range(depth):
        w_main, w_gate, w_uq, gains = _layer_params(
            l, w_in, sb_q_hnorm, sb_k_hnorm, mla_w_uq, mla_q_hnorm, mla_k_hnorm, mem_q_hnorm)

        x2d = _ffn(x.reshape(n, d), row(ffn1_norm[l]), ffn1_w_in[l].astype(BF16), ffn1_w_out[l].astype(BF16))

        y, sbq, sbk, sbv, mq, mk, mv, memq = _proj(
            x2d, row(mix_norm[l]), w_main, cos_t, sin_t, row(mla_q_norm[l]), row(mla_kv_norm[l]), w_uq,
            mla_w_ukv[l].astype(BF16), gains)
        seq = lambda a: a.reshape(bsz, s, a.shape[-1])

        conv_o = _conv_branch(seq(y), conv_dw[l], row(conv_b[l]), row(conv_ln_g[l]), row(conv_ln_b[l]))
        sb_o = _sb_attention(seq(sbq), seq(sbk), seq(sbv))
        mla_o = _mla_attention(seq(mq), seq(mk), seq(mv))
        mem_k, mem_v = _memkv(mem, row(mem_norm[l]), mem_w_kv[l].astype(BF16), row(mem_k_hnorm[l]))

        x = _merge(seq(x2d), row(mix_norm[l]), w_gate, w_branch[l].astype(BF16), w_out[l].astype(BF16),
                   conv_o, sb_o, mla_o, seq(memq), mem_k, mem_v)

        x = _ffn(x.reshape(n, d), row(ffn2_norm[l]), ffn2_w_in[l].astype(BF16),
                 ffn2_w_out[l].astype(BF16)).reshape(bsz, s, d)
    return x
```

```python
import functools

import jax
import jax.numpy as jnp
from jax import lax
from jax.experimental import pallas as pl
from jax.experimental.pallas import tpu as pltpu

F32 = jnp.float32
BF16 = jnp.bfloat16

D_MODEL = 1024
FFN_HIDDEN = 2048
CHUNK = 64
CONV_CH = 512
CONV_WIDTH = 31
HEADS = 4
HEAD_DIM = 128
MLA_NOPE = 128
MLA_ROPE = 64
MLA_QK_PAD = 256
MLA_Q_LORA = 256
MLA_KV_LORA = 256
N_BRANCH = 4
BRANCH_WIDTH = 512
ROPE_BASE = 10000.0
EPS = 1e-6
MASK_VALUE = -1e30
LOG2_E = 1.4426950408889634
SB_UNDERFLOW_LOG = -110.0

LANES = 128
SUBLANES = 8
CONV_HALO = 32
V7X_VMEM_LIMIT_BYTES = 56 * 1024 * 1024

_C_CONV = 0
_C_SBQ = 1024
_C_SBK = 1536
_C_SBV = 2048
_C_QLAT = 2560
_C_KVLAT = 2816
_C_KR = 3072
_C_KRSW = 3200
_C_MEMQ = 3328
_C_END = 3840


def _params(*sem):
    return pltpu.CompilerParams(dimension_semantics=sem, vmem_limit_bytes=V7X_VMEM_LIMIT_BYTES)


def _resident(shape):
    nd = len(shape)
    return pl.BlockSpec(shape, lambda *_: (0,) * nd)


def _rms_scale(x):
    return lax.rsqrt(jnp.mean(x * x, axis=-1, keepdims=True) + EPS)


FFN_ROWS = 512
FFN_COLS = 512


def _ffn_kernel(x_ref, g_ref, win_ref, wout_ref, o_ref, act_ref):
    x = x_ref[...]
    h = (x * _rms_scale(x) * g_ref[...]).astype(BF16)
    for c in range(FFN_HIDDEN // FFN_COLS):
        lo = c * FFN_COLS
        gate = jnp.dot(h, win_ref[:, lo:lo + FFN_COLS], preferred_element_type=F32)
        up = jnp.dot(h, win_ref[:, FFN_HIDDEN + lo:FFN_HIDDEN + lo + FFN_COLS], preferred_element_type=F32)
        act_ref[:, lo:lo + FFN_COLS] = (gate * jax.nn.sigmoid(gate) * up).astype(BF16)
    y = jnp.dot(act_ref[...], wout_ref[...], preferred_element_type=F32)
    o_ref[...] = x + 0.5 * y


def _ffn(x2d, g, w_in, w_out):
    n, d = x2d.shape
    tm = min(FFN_ROWS, n)
    return pl.pallas_call(
        _ffn_kernel,
        grid=(n // tm,),
        in_specs=[
            pl.BlockSpec((tm, d), lambda i: (i, 0)),
            _resident(g.shape),
            _resident(w_in.shape),
            _resident(w_out.shape),
        ],
        out_specs=pl.BlockSpec((tm, d), lambda i: (i, 0)),
        out_shape=jax.ShapeDtypeStruct((n, d), F32),
        scratch_shapes=[pltpu.VMEM((tm, FFN_HIDDEN), BF16)],
        compiler_params=_params("parallel"),
        name="ffn_half_step",
    )(x2d, g, w_in, w_out)


PROJ_ROWS = 256


def _head_sumsq(x):
    return jnp.sum(x * x, axis=-1, keepdims=True)


def _proj_kernel(x_ref, mixg_ref, w_ref, cos_ref, sin_ref, qn_ref, kvn_ref, wuq_ref, wukv_ref, gains_ref,
                 y_ref, sbq_ref, sbk_ref, sbv_ref, mq_ref, mk_ref, mv_ref, memq_ref):
    x = x_ref[...]
    h = (x * _rms_scale(x) * mixg_ref[...]).astype(BF16)
    u = jnp.dot(h, w_ref[...], preferred_element_type=F32)
    gains = gains_ref[...]
    sb_qg, sb_kg, mem_qg = gains[0:1], gains[1:2], gains[2:3]
    q_ga, q_gb, q_gbsw = gains[3:4], gains[4:5], gains[5:6]
    k_ga, k_gb, k_gbsw = gains[6:7], gains[7:8], gains[8:9]
    cos_t = cos_ref[...]
    sin_t = sin_ref[...]

    y_ref[...] = (u[:, _C_CONV:_C_CONV + CONV_CH]
                  * jax.nn.sigmoid(u[:, _C_CONV + CONV_CH:_C_CONV + 2 * CONV_CH])).astype(BF16)

    for hd in range(HEADS):
        lo = hd * HEAD_DIM
        q = u[:, _C_SBQ + lo:_C_SBQ + lo + HEAD_DIM]
        k = u[:, _C_SBK + lo:_C_SBK + lo + HEAD_DIM]
        m = u[:, _C_MEMQ + lo:_C_MEMQ + lo + HEAD_DIM]
        sbq_ref[:, lo:lo + HEAD_DIM] = (q * lax.rsqrt(_head_sumsq(q) / HEAD_DIM + EPS) * sb_qg).astype(BF16)
        sbk_ref[:, lo:lo + HEAD_DIM] = (k * lax.rsqrt(_head_sumsq(k) / HEAD_DIM + EPS) * sb_kg).astype(BF16)
        memq_ref[:, lo:lo + HEAD_DIM] = (m * lax.rsqrt(_head_sumsq(m) / HEAD_DIM + EPS) * mem_qg).astype(BF16)
    sbv_ref[...] = u[:, _C_SBV:_C_SBV + HEADS * HEAD_DIM].astype(BF16)

    q_lat = u[:, _C_QLAT:_C_QLAT + MLA_Q_LORA]
    kv_lat = u[:, _C_KVLAT:_C_KVLAT + MLA_KV_LORA]
    q_lat_n = (q_lat * _rms_scale(q_lat) * qn_ref[...]).astype(BF16)
    kv_lat_n = (kv_lat * _rms_scale(kv_lat) * kvn_ref[...]).astype(BF16)
    q_up = jnp.dot(q_lat_n, wuq_ref[...], preferred_element_type=F32)
    kv_up = jnp.dot(kv_lat_n, wukv_ref[...], preferred_element_type=F32)
    kr = u[:, _C_KR:_C_KR + LANES]
    kr_sw = u[:, _C_KRSW:_C_KRSW + LANES]
    kr_ss = _head_sumsq(kr)
    qk_dim = MLA_NOPE + MLA_ROPE
    for hd in range(HEADS):
        qa = q_up[:, hd * 384:hd * 384 + LANES]
        qb = q_up[:, hd * 384 + LANES:hd * 384 + 2 * LANES]
        qbsw = q_up[:, hd * 384 + 2 * LANES:hd * 384 + 3 * LANES]
        qr = lax.rsqrt((_head_sumsq(qa) + _head_sumsq(qb)) / qk_dim + EPS)
        mq_ref[:, hd * MLA_QK_PAD:hd * MLA_QK_PAD + LANES] = (qa * qr * q_ga).astype(BF16)
        mq_ref[:, hd * MLA_QK_PAD + LANES:(hd + 1) * MLA_QK_PAD] = (
            (qb * q_gb * cos_t + qbsw * q_gbsw * sin_t) * qr).astype(BF16)
        ka = kv_up[:, hd * 256:hd * 256 + LANES]
        kr_inv = lax.rsqrt((_head_sumsq(ka) + kr_ss) / qk_dim + EPS)
        mk_ref[:, hd * MLA_QK_PAD:hd * MLA_QK_PAD + LANES] = (ka * kr_inv * k_ga).astype(BF16)
        mk_ref[:, hd * MLA_QK_PAD + LANES:(hd + 1) * MLA_QK_PAD] = (
            (kr * k_gb * cos_t + kr_sw * k_gbsw * sin_t) * kr_inv).astype(BF16)
        mv_ref[:, hd * HEAD_DIM:(hd + 1) * HEAD_DIM] = kv_up[:, hd * 256 + LANES:(hd + 1) * 256].astype(BF16)


def _proj(x2d, mix_g, w_main, cos_t, sin_t, q_norm, kv_norm, w_uq, w_ukv, gains):
    n, d = x2d.shape
    tm = min(PROJ_ROWS, n)
    row = lambda w: pl.BlockSpec((tm, w), lambda i: (i, 0))
    widths = (CONV_CH, 512, 512, 512, HEADS * MLA_QK_PAD, HEADS * MLA_QK_PAD, 512, 512)
    return pl.pallas_call(
        _proj_kernel,
        grid=(n // tm,),
        in_specs=[row(d), _resident(mix_g.shape), _resident(w_main.shape), row(LANES), row(LANES),
                  _resident(q_norm.shape), _resident(kv_norm.shape), _resident(w_uq.shape),
                  _resident(w_ukv.shape), _resident(gains.shape)],
        out_specs=[row(w) for w in widths],
        out_shape=[jax.ShapeDtypeStruct((n, w), BF16) for w in widths],
        compiler_params=_params("parallel"),
        name="mix_in_proj",
    )(x2d, mix_g, w_main, cos_t, sin_t, q_norm, kv_norm, w_uq, w_ukv, gains)


CONV_ROWS = 256
CONV_CHUNK = 64


def _conv_kernel(y_ref, halo_ref, dw_ref, b_ref, lng_ref, lnb_ref, o_ref, sh_ref):
    j = pl.program_id(1)
    tm = y_ref.shape[1]
    halo = halo_ref[0].astype(F32)
    sh_ref[0, 0:CONV_HALO, :] = jnp.where(j > 0, halo, jnp.zeros_like(halo))
    sh_ref[0, CONV_HALO:, :] = y_ref[0].astype(F32)
    span = tm + CONV_HALO - SUBLANES
    for r in range(1, SUBLANES):
        sh_ref[r, 0:span, :] = sh_ref[0, r:r + span, :]
    dw = dw_ref[...]
    first = CONV_HALO - (CONV_WIDTH - 1)
    for c in range(tm // CONV_CHUNK):
        r0 = c * CONV_CHUNK
        acc = jnp.zeros((CONV_CHUNK, CONV_CH), F32) + b_ref[...]
        for k in range(CONV_WIDTH):
            phase, base = (first + k) % SUBLANES, (first + k) // SUBLANES * SUBLANES
            acc = acc + dw[k:k + 1, :] * sh_ref[phase, r0 + base:r0 + base + CONV_CHUNK, :]
        mu = jnp.mean(acc, axis=-1, keepdims=True)
        cen = acc - mu
        var = jnp.mean(cen * cen, axis=-1, keepdims=True)
        z = cen * lax.rsqrt(var + EPS) * lng_ref[...] + lnb_ref[...]
        o_ref[0, r0:r0 + CONV_CHUNK, :] = (z * jax.nn.sigmoid(z)).astype(BF16)


def _conv_branch(y, dw, b, ln_g, ln_b):
    bsz, s, ch = y.shape
    tm = min(CONV_ROWS, s)
    ratio = tm // CONV_HALO
    return pl.pallas_call(
        _conv_kernel,
        grid=(bsz, s // tm),
        in_specs=[
            pl.BlockSpec((1, tm, ch), lambda b_, j: (b_, j, 0)),
            pl.BlockSpec((1, CONV_HALO, ch), lambda b_, j: (b_, jnp.maximum(j * ratio - 1, 0), 0)),
            _resident(dw.shape), _resident(b.shape), _resident(ln_g.shape), _resident(ln_b.shape),
        ],
        out_specs=pl.BlockSpec((1, tm, ch), lambda b_, j: (b_, j, 0)),
        out_shape=jax.ShapeDtypeStruct((bsz, s, ch), BF16),
        scratch_shapes=[pltpu.VMEM((SUBLANES, tm + CONV_HALO, ch), F32)],
        compiler_params=_params("parallel", "parallel"),
        name="conformer_conv",
    )(y, y, dw, b, ln_g, ln_b)


ATT_BLOCK = 256


def _tn_dot(a, b):
    return lax.dot_general(a, b, (((0,), (0,)), ((), ())), preferred_element_type=F32)


def _sb_block(k_blk, v_blk, q_t, ut, carry, out_t, strict):
    z = jnp.dot(k_blk, q_t, preferred_element_type=F32)
    log_keep = -(jnp.maximum(z, 0.0) + jnp.log(1.0 + jnp.exp(-jnp.abs(z))))
    if strict is not None:
        log_keep = jnp.where(strict, log_keep, 0.0)
    later = jnp.dot(ut, log_keep.astype(BF16), preferred_element_type=F32) + carry
    logw = z + log_keep + later
    if strict is not None:
        logw = jnp.where(strict, logw, MASK_VALUE)
    w = jnp.exp(logw)
    out_t = out_t + _tn_dot(v_blk, w.astype(BF16))
    carry = carry + jnp.sum(log_keep, axis=0, keepdims=True)
    return carry, out_t


def _sb_kernel(q_ref, k_ref, v_ref, o_ref, qt_ref, ut_ref, carry_ref, acc_ref):
    i = pl.program_id(1)
    blk = q_ref.shape[1]
    key = lax.broadcasted_iota(jnp.int32, (blk, blk), 0)
    col = lax.broadcasted_iota(jnp.int32, (blk, blk), 1)
    strict = key < col
    ut_ref[...] = (col > key).astype(BF16)
    for hd in range(HEADS):
        qt_ref[hd] = q_ref[0, :, hd * HEAD_DIM:(hd + 1) * HEAD_DIM].astype(F32).T.astype(BF16)
    carry_ref[...] = jnp.zeros(carry_ref.shape, F32)
    acc_ref[...] = jnp.zeros(acc_ref.shape, F32)

    def block(kb, mask):
        start = pl.multiple_of(kb * blk, blk)
        top = None
        for hd in range(HEADS):
            cols = slice(hd * HEAD_DIM, (hd + 1) * HEAD_DIM)
            carry, out_t = _sb_block(k_ref[0, pl.ds(start, blk), cols], v_ref[0, pl.ds(start, blk), cols],
                                     qt_ref[hd], ut_ref[...], carry_ref[hd], acc_ref[hd], mask)
            carry_ref[hd] = carry
            acc_ref[hd] = out_t
            top = carry if top is None else jnp.maximum(top, carry)
        return jnp.max(top)

    top = block(i, strict)

    def more(state):
        r, top = state
        return jnp.logical_and(r < i, top > SB_UNDERFLOW_LOG)

    def body(state):
        r, _ = state
        return r + 1, block(i - 1 - r, None)

    lax.while_loop(more, body, (jnp.int32(0), top))
    for hd in range(HEADS):
        o_ref[0, :, hd * HEAD_DIM:(hd + 1) * HEAD_DIM] = acc_ref[hd].T.astype(BF16)


def _sb_attention(q, k, v):
    bsz, s, width = q.shape
    blk = min(ATT_BLOCK, s)
    return pl.pallas_call(
        _sb_kernel,
        grid=(bsz, s // blk),
        in_specs=[
            pl.BlockSpec((1, blk, width), lambda b, i: (b, i, 0)),
            pl.BlockSpec((1, s, width), lambda b, i: (b, 0, 0)),
            pl.BlockSpec((1, s, width), lambda b, i: (b, 0, 0)),
        ],
        out_specs=pl.BlockSpec((1, blk, width), lambda b, i: (b, i, 0)),
        out_shape=jax.ShapeDtypeStruct((bsz, s, width), BF16),
        scratch_shapes=[
            pltpu.VMEM((HEADS, HEAD_DIM, blk), BF16),
            pltpu.VMEM((blk, blk), BF16),
            pltpu.VMEM((HEADS, 1, blk), F32),
            pltpu.VMEM((HEADS, HEAD_DIM, blk), F32),
        ],
        compiler_params=_params("parallel", "arbitrary"),
        name="stick_breaking_attention",
    )(q, k, v)


MLA_Q_TILE = 1024
MLA_K_TILE = 512


def _mla_kernel(q_ref, k_ref, v_ref, o_ref, qt_ref, sa_ref, sb_ref, m_ref, l_ref, acc_ref):
    i = pl.program_id(2)
    tq, tk = q_ref.shape[1], sa_ref.shape[0]
    per_q = tq // tk
    qt_ref[...] = q_ref[0].astype(F32).T.astype(BF16)
    m_ref[...] = jnp.full(m_ref.shape, MASK_VALUE, F32)
    l_ref[...] = jnp.zeros(l_ref.shape, F32)
    acc_ref[...] = jnp.zeros(acc_ref.shape, F32)

    def scores(kb, dst_ref):
        start = pl.multiple_of(kb * tk, tk)
        dst_ref[...] = jnp.dot(k_ref[0, pl.ds(start, tk), :], qt_ref[...], preferred_element_type=F32)

    def update(src_ref, kb, allowed):
        start = pl.multiple_of(kb * tk, tk)
        s = src_ref[...]
        if allowed is not None:
            s = jnp.where(allowed, s, MASK_VALUE)
        m_prev = m_ref[...]
        m_new = jnp.maximum(m_prev, jnp.max(s, axis=0, keepdims=True))
        alpha = jnp.exp2(m_prev - m_new)
        p = jnp.exp2(s - m_new)
        l_ref[...] = alpha * l_ref[...] + jnp.sum(p, axis=0, keepdims=True)
        acc_ref[...] = alpha * acc_ref[...] + _tn_dot(v_ref[0, pl.ds(start, tk), :], p.astype(BF16))
        m_ref[...] = m_new

    sa_ref[...] = jnp.dot(k_ref[0, 0:tk, :], qt_ref[...], preferred_element_type=F32)

    def pair(p, carry):
        scores(2 * p + 1, sb_ref)
        update(sa_ref, 2 * p, None)
        scores(2 * p + 2, sa_ref)
        update(sb_ref, 2 * p + 1, None)
        return carry

    lax.fori_loop(0, i * (per_q // 2), pair, 0)

    key = lax.broadcasted_iota(jnp.int32, (tk, tq), 0)
    col_chunk = lax.broadcasted_iota(jnp.int32, (tk, tq), 1) // CHUNK
    first = i * per_q
    for d in range(per_q):
        if d + 1 < per_q:
            scores(first + d + 1, sb_ref if d % 2 == 0 else sa_ref)
        update(sa_ref if d % 2 == 0 else sb_ref, first + d, (key + d * tk) // CHUNK <= col_chunk)
    o_ref[0] = (acc_ref[...] / l_ref[...]).T.astype(BF16)


def _mla_attention(q, k, v):
    bsz, s, _ = q.shape
    tq = min(MLA_Q_TILE, s)
    tk = min(MLA_K_TILE, tq // 2)
    assert tq == 2 * tk and s % tq == 0
    return pl.pallas_call(
        _mla_kernel,
        grid=(bsz, HEADS, s // tq),
        in_specs=[
            pl.BlockSpec((1, tq, MLA_QK_PAD), lambda b, h, i: (b, i, h)),
            pl.BlockSpec((1, s, MLA_QK_PAD), lambda b, h, i: (b, 0, h)),
            pl.BlockSpec((1, s, HEAD_DIM), lambda b, h, i: (b, 0, h)),
        ],
        out_specs=pl.BlockSpec((1, tq, HEAD_DIM), lambda b, h, i: (b, i, h)),
        out_shape=jax.ShapeDtypeStruct((bsz, s, HEADS * HEAD_DIM), BF16),
        scratch_shapes=[
            pltpu.VMEM((MLA_QK_PAD, tq), BF16),
            pltpu.VMEM((tk, tq), F32),
            pltpu.VMEM((tk, tq), F32),
            pltpu.VMEM((1, tq), F32),
            pltpu.VMEM((1, tq), F32),
            pltpu.VMEM((HEAD_DIM, tq), F32),
        ],
        compiler_params=_params("parallel", "parallel", "arbitrary"),
        name="chunk_causal_latent_attention",
    )(q, k, v)


def _memkv_kernel(mem_ref, g_ref, w_ref, kg_ref, k_ref, v_ref):
    m = mem_ref[0]
    mn = (m * _rms_scale(m) * g_ref[...]).astype(BF16)
    kv = jnp.dot(mn, w_ref[...], preferred_element_type=F32)
    for hd in range(HEADS):
        lo = hd * HEAD_DIM
        k = kv[:, lo:lo + HEAD_DIM]
        k_ref[0, :, lo:lo + HEAD_DIM] = (k * lax.rsqrt(_head_sumsq(k) / HEAD_DIM + EPS) * kg_ref[...]).astype(BF16)
    v_ref[0] = kv[:, HEADS * HEAD_DIM:].astype(BF16)


def _memkv(mem, g, w_kv, k_gain):
    bsz, m, d = mem.shape
    width = HEADS * HEAD_DIM
    blk = lambda w: pl.BlockSpec((1, m, w), lambda b: (b, 0, 0))
    return pl.pallas_call(
        _memkv_kernel,
        grid=(bsz,),
        in_specs=[blk(d), _resident(g.shape), _resident(w_kv.shape), _resident(k_gain.shape)],
        out_specs=[blk(width), blk(width)],
        out_shape=[jax.ShapeDtypeStruct((bsz, m, width), BF16)] * 2,
        compiler_params=_params("parallel"),
        name="memory_kv",
    )(mem, g, w_kv, k_gain)


MERGE_ROWS = 256


def _merge_kernel(x_ref, mixg_ref, wg_ref, wb_ref, wo_ref, conv_ref, sb_ref, mla_ref, memq_ref, memk_ref, memv_ref,
                  o_ref, memo_ref):
    x = x_ref[0]
    h = (x * _rms_scale(x) * mixg_ref[...]).astype(BF16)

    for hd in range(HEADS):
        lo = hd * HEAD_DIM
        s = lax.dot_general(memq_ref[0, :, lo:lo + HEAD_DIM], memk_ref[0, :, lo:lo + HEAD_DIM],
                            (((1,), (1,)), ((), ())), preferred_element_type=F32)
        p = jnp.exp(s - jnp.max(s, axis=-1, keepdims=True))
        o = jnp.dot(p.astype(BF16), memv_ref[0, :, lo:lo + HEAD_DIM], preferred_element_type=F32)
        memo_ref[:, lo:lo + HEAD_DIM] = (o / jnp.sum(p, axis=-1, keepdims=True)).astype(BF16)

    branches = (conv_ref[0], sb_ref[0], mla_ref[0], memo_ref[...])
    merged = None
    for i, br in enumerate(branches):
        gate = jax.nn.sigmoid(jnp.dot(h, wg_ref[:, i * D_MODEL:(i + 1) * D_MODEL], preferred_element_type=F32))
        term = gate * jnp.dot(br, wb_ref[i], preferred_element_type=F32)
        merged = term if merged is None else merged + term
    o_ref[0] = x + jnp.dot(merged.astype(BF16), wo_ref[...], preferred_element_type=F32)


def _merge(x, mix_g, w_gate, w_branch, w_out, conv_o, sb_o, mla_o, mem_q, mem_k, mem_v):
    bsz, s, d = x.shape
    tm = min(MERGE_ROWS, s)
    row = lambda w: pl.BlockSpec((1, tm, w), lambda b, j: (b, j, 0))
    mem = pl.BlockSpec((1,) + mem_k.shape[1:], lambda b, j: (b, 0, 0))
    return pl.pallas_call(
        _merge_kernel,
        grid=(bsz, s // tm),
        in_specs=[row(d), _resident(mix_g.shape), _resident(w_gate.shape), _resident(w_branch.shape),
                  _resident(w_out.shape), row(BRANCH_WIDTH), row(BRANCH_WIDTH), row(BRANCH_WIDTH),
                  row(BRANCH_WIDTH), mem, mem],
        out_specs=row(d),
        out_shape=jax.ShapeDtypeStruct((bsz, s, d), F32),
        scratch_shapes=[pltpu.VMEM((tm, BRANCH_WIDTH), BF16)],
        compiler_params=_params("parallel", "parallel"),
        name="gated_merge_out_proj",
    )(x, mix_g, w_gate, w_branch, w_out, conv_o, sb_o, mla_o, mem_q, mem_k, mem_v)


def _pad_lanes(a, width=LANES):
    return jnp.pad(a, [(0, 0)] * (a.ndim - 1) + [(0, width - a.shape[-1])])


def _swap_halves(a):
    half = a.shape[-1] // 2
    return jnp.concatenate([a[..., half:], a[..., :half]], axis=-1)


def _rope_tables(positions):
    inv_freq = ROPE_BASE ** (-jnp.arange(0, MLA_ROPE, 2, dtype=F32) / MLA_ROPE)
    ang = positions.astype(F32)[..., None] * inv_freq
    cos, sin = jnp.cos(ang), jnp.sin(ang)
    cos_t = _pad_lanes(jnp.concatenate([cos, cos], axis=-1))
    sin_t = _pad_lanes(jnp.concatenate([-sin, sin], axis=-1))
    return cos_t, sin_t


def _layer_params(l, w_in, sb_q_hnorm, sb_k_hnorm, mla_w_uq, mla_q_hnorm, mla_k_hnorm, mem_q_hnorm):
    w = w_in[l]
    kr = w[:, 3072:3136]
    w_main = jnp.concatenate(
        [w[:, :3072], _pad_lanes(kr), _pad_lanes(_swap_halves(kr)), w[:, 3136:3648]], axis=1).astype(BF16)
    w_gate = w[:, 3648:].astype(BF16)

    wuq = mla_w_uq[l].reshape(MLA_Q_LORA, HEADS, MLA_NOPE + MLA_ROPE)
    rot = wuq[..., MLA_NOPE:]
    w_uq = jnp.concatenate([wuq[..., :MLA_NOPE], _pad_lanes(rot), _pad_lanes(_swap_halves(rot))], axis=-1)
    w_uq = w_uq.reshape(MLA_Q_LORA, HEADS * 3 * LANES).astype(BF16)

    sb_scale = HEAD_DIM ** -0.5
    mla_scale = (MLA_NOPE + MLA_ROPE) ** -0.5 * LOG2_E
    qg, kg = mla_q_hnorm[l] * mla_scale, mla_k_hnorm[l]
    rows = [sb_q_hnorm[l] * sb_scale, sb_k_hnorm[l], mem_q_hnorm[l] * sb_scale,
            qg[:MLA_NOPE], _pad_lanes(qg[MLA_NOPE:]), _pad_lanes(_swap_halves(qg[MLA_NOPE:])),
            kg[:MLA_NOPE], _pad_lanes(kg[MLA_NOPE:]), _pad_lanes(_swap_halves(kg[MLA_NOPE:]))]
    gains = jnp.stack(rows + [jnp.zeros((LANES,), F32)] * (16 - len(rows)))
    return w_main, w_gate, w_uq, gains


def kernel(x, mem, positions, ffn1_norm, ffn1_w_in, ffn1_w_out, mix_norm, w_in, conv_dw, conv_b, conv_ln_g,
           conv_ln_b, sb_q_hnorm, sb_k_hnorm, mla_q_norm, mla_w_uq, mla_kv_norm, mla_w_ukv, mla_q_hnorm,
           mla_k_hnorm, mem_norm, mem_w_kv, mem_q_hnorm, mem_k_hnorm, w_branch, w_out, ffn2_norm, ffn2_w_in,
           ffn2_w_out):
    bsz, s, d = x.shape
    depth = w_in.shape[0]
    n = bsz * s
    cos_t, sin_t = _rope_tables(positions)
    cos_t, sin_t = cos_t.reshape(n, LANES), sin_t.reshape(n, LANES)
    row = lambda a: a.reshape(1, -1)

    for l in range(depth):
        w_main, w_gate, w_uq, gains = _layer_params(
            l, w_in, sb_q_hnorm, sb_k_hnorm, mla_w_uq, mla_q_hnorm, mla_k_hnorm, mem_q_hnorm)

        x2d = _ffn(x.reshape(n, d), row(ffn1_norm[l]), ffn1_w_in[l].astype(BF16), ffn1_w_out[l].astype(BF16))

        y, sbq, sbk, sbv, mq, mk, mv, memq = _proj(
            x2d, row(mix_norm[l]), w_main, cos_t, sin_t, row(mla_q_norm[l]), row(mla_kv_norm[l]), w_uq,
            mla_w_ukv[l].astype(BF16), gains)
        seq = lambda a: a.reshape(bsz, s, a.shape[-1])

        conv_o = _conv_branch(seq(y), conv_dw[l], row(conv_b[l]), row(conv_ln_g[l]), row(conv_ln_b[l]))
        sb_o = _sb_attention(seq(sbq), seq(sbk), seq(sbv))
        mla_o = _mla_attention(seq(mq), seq(mk), seq(mv))
        mem_k, mem_v = _memkv(mem, row(mem_norm[l]), mem_w_kv[l].astype(BF16), row(mem_k_hnorm[l]))

        x = _merge(seq(x2d), row(mix_norm[l]), w_gate, w_branch[l].astype(BF16), w_out[l].astype(BF16),
                   conv_o, sb_o, mla_o, seq(memq), mem_k, mem_v)

        x = _ffn(x.reshape(n, d), row(ffn2_norm[l]), ffn2_w_in[l].astype(BF16),
                 ffn2_w_out[l].astype(BF16)).reshape(bsz, s, d)
    return x
```

```python
import functools

import jax
import jax.numpy as jnp
from jax import lax
from jax.experimental import pallas as pl
from jax.experimental.pallas import tpu as pltpu

F32 = jnp.float32
BF16 = jnp.bfloat16

D_MODEL = 1024
FFN_HIDDEN = 2048
CHUNK = 64
CONV_CH = 512
CONV_WIDTH = 31
HEADS = 4
HEAD_DIM = 128
MLA_NOPE = 128
MLA_ROPE = 64
MLA_QK_PAD = 256
MLA_Q_LORA = 256
MLA_KV_LORA = 256
N_BRANCH = 4
BRANCH_WIDTH = 512
ROPE_BASE = 10000.0
EPS = 1e-6
MASK_VALUE = -1e30
LOG2_E = 1.4426950408889634
SB_UNDERFLOW_LOG = -110.0

LANES = 128
SUBLANES = 8
CONV_HALO = 32
V7X_VMEM_LIMIT_BYTES = 56 * 1024 * 1024

_C_CONV = 0
_C_SBQ = 1024
_C_SBK = 1536
_C_SBV = 2048
_C_QLAT = 2560
_C_KVLAT = 2816
_C_KR = 3072
_C_KRSW = 3200
_C_MEMQ = 3328
_C_END = 3840


def _params(*sem):
    return pltpu.CompilerParams(dimension_semantics=sem, vmem_limit_bytes=V7X_VMEM_LIMIT_BYTES)


def _resident(shape):
    nd = len(shape)
    return pl.BlockSpec(shape, lambda *_: (0,) * nd)


def _rms_scale(x):
    return lax.rsqrt(jnp.mean(x * x, axis=-1, keepdims=True) + EPS)


FFN_ROWS = 512
FFN_COLS = 512


def _ffn_kernel(x_ref, g_ref, win_ref, wout_ref, o_ref, act_ref):
    x = x_ref[...]
    h = (x * _rms_scale(x) * g_ref[...]).astype(BF16)
    for c in range(FFN_HIDDEN // FFN_COLS):
        lo = c * FFN_COLS
        gate = jnp.dot(h, win_ref[:, lo:lo + FFN_COLS], preferred_element_type=F32)
        up = jnp.dot(h, win_ref[:, FFN_HIDDEN + lo:FFN_HIDDEN + lo + FFN_COLS], preferred_element_type=F32)
        act_ref[:, lo:lo + FFN_COLS] = (gate * jax.nn.sigmoid(gate) * up).astype(BF16)
    y = jnp.dot(act_ref[...], wout_ref[...], preferred_element_type=F32)
    o_ref[...] = x + 0.5 * y


def _ffn(x2d, g, w_in, w_out):
    n, d = x2d.shape
    tm = min(FFN_ROWS, n)
    return pl.pallas_call(
        _ffn_kernel,
        grid=(n // tm,),
        in_specs=[
            pl.BlockSpec((tm, d), lambda i: (i, 0)),
            _resident(g.shape),
            _resident(w_in.shape),
            _resident(w_out.shape),
        ],
        out_specs=pl.BlockSpec((tm, d), lambda i: (i, 0)),
        out_shape=jax.ShapeDtypeStruct((n, d), F32),
        scratch_shapes=[pltpu.VMEM((tm, FFN_HIDDEN), BF16)],
        compiler_params=_params("parallel"),
        name="ffn_half_step",
    )(x2d, g, w_in, w_out)


PROJ_ROWS = 512


def _head_sumsq(x):
    return jnp.sum(x * x, axis=-1, keepdims=True)


def _proj_kernel(x_ref, mixg_ref, w_ref, cos_ref, sin_ref, qn_ref, kvn_ref, wuq_ref, wukv_ref, gains_ref,
                 y_ref, sbq_ref, sbk_ref, sbv_ref, mq_ref, mk_ref, mv_ref, memq_ref):
    x = x_ref[...]
    h = (x * _rms_scale(x) * mixg_ref[...]).astype(BF16)
    u = jnp.dot(h, w_ref[...], preferred_element_type=F32)
    gains = gains_ref[...]
    sb_qg, sb_kg, mem_qg = gains[0:1], gains[1:2], gains[2:3]
    q_ga, q_gb, q_gbsw = gains[3:4], gains[4:5], gains[5:6]
    k_ga, k_gb, k_gbsw = gains[6:7], gains[7:8], gains[8:9]
    cos_t = cos_ref[...]
    sin_t = sin_ref[...]

    y_ref[...] = (u[:, _C_CONV:_C_CONV + CONV_CH]
                  * jax.nn.sigmoid(u[:, _C_CONV + CONV_CH:_C_CONV + 2 * CONV_CH])).astype(BF16)

    for hd in range(HEADS):
        lo = hd * HEAD_DIM
        q = u[:, _C_SBQ + lo:_C_SBQ + lo + HEAD_DIM]
        k = u[:, _C_SBK + lo:_C_SBK + lo + HEAD_DIM]
        m = u[:, _C_MEMQ + lo:_C_MEMQ + lo + HEAD_DIM]
        sbq_ref[:, lo:lo + HEAD_DIM] = (q * lax.rsqrt(_head_sumsq(q) / HEAD_DIM + EPS) * sb_qg).astype(BF16)
        sbk_ref[:, lo:lo + HEAD_DIM] = (k * lax.rsqrt(_head_sumsq(k) / HEAD_DIM + EPS) * sb_kg).astype(BF16)
        memq_ref[:, lo:lo + HEAD_DIM] = (m * lax.rsqrt(_head_sumsq(m) / HEAD_DIM + EPS) * mem_qg).astype(BF16)
    sbv_ref[...] = u[:, _C_SBV:_C_SBV + HEADS * HEAD_DIM].astype(BF16)

    q_lat = u[:, _C_QLAT:_C_QLAT + MLA_Q_LORA]
    kv_lat = u[:, _C_KVLAT:_C_KVLAT + MLA_KV_LORA]
    q_lat_n = (q_lat * _rms_scale(q_lat) * qn_ref[...]).astype(BF16)
    kv_lat_n = (kv_lat * _rms_scale(kv_lat) * kvn_ref[...]).astype(BF16)
    q_up = jnp.dot(q_lat_n, wuq_ref[...], preferred_element_type=F32)
    kv_up = jnp.dot(kv_lat_n, wukv_ref[...], preferred_element_type=F32)
    kr = u[:, _C_KR:_C_KR + LANES]
    kr_sw = u[:, _C_KRSW:_C_KRSW + LANES]
    kr_ss = _head_sumsq(kr)
    qk_dim = MLA_NOPE + MLA_ROPE
    for hd in range(HEADS):
        qa = q_up[:, hd * 384:hd * 384 + LANES]
        qb = q_up[:, hd * 384 + LANES:hd * 384 + 2 * LANES]
        qbsw = q_up[:, hd * 384 + 2 * LANES:hd * 384 + 3 * LANES]
        qr = lax.rsqrt((_head_sumsq(qa) + _head_sumsq(qb)) / qk_dim + EPS)
        mq_ref[:, hd * MLA_QK_PAD:hd * MLA_QK_PAD + LANES] = (qa * qr * q_ga).astype(BF16)
        mq_ref[:, hd * MLA_QK_PAD + LANES:(hd + 1) * MLA_QK_PAD] = (
            (qb * q_gb * cos_t + qbsw * q_gbsw * sin_t) * qr).astype(BF16)
        ka = kv_up[:, hd * 256:hd * 256 + LANES]
        kr_inv = lax.rsqrt((_head_sumsq(ka) + kr_ss) / qk_dim + EPS)
        mk_ref[:, hd * MLA_QK_PAD:hd * MLA_QK_PAD + LANES] = (ka * kr_inv * k_ga).astype(BF16)
        mk_ref[:, hd * MLA_QK_PAD + LANES:(hd + 1) * MLA_QK_PAD] = (
            (kr * k_gb * cos_t + kr_sw * k_gbsw * sin_t) * kr_inv).astype(BF16)
        mv_ref[:, hd * HEAD_DIM:(hd + 1) * HEAD_DIM] = kv_up[:, hd * 256 + LANES:(hd + 1) * 256].astype(BF16)


def _proj(x2d, mix_g, w_main, cos_t, sin_t, q_norm, kv_norm, w_uq, w_ukv, gains):
    n, d = x2d.shape
    tm = min(PROJ_ROWS, n)
    row = lambda w: pl.BlockSpec((tm, w), lambda i: (i, 0))
    widths = (CONV_CH, 512, 512, 512, HEADS * MLA_QK_PAD, HEADS * MLA_QK_PAD, 512, 512)
    return pl.pallas_call(
        _proj_kernel,
        grid=(n // tm,),
        in_specs=[row(d), _resident(mix_g.shape), _resident(w_main.shape), row(LANES), row(LANES),
                  _resident(q_norm.shape), _resident(kv_norm.shape), _resident(w_uq.shape),
                  _resident(w_ukv.shape), _resident(gains.shape)],
        out_specs=[row(w) for w in widths],
        out_shape=[jax.ShapeDtypeStruct((n, w), BF16) for w in widths],
        compiler_params=_params("parallel"),
        name="mix_in_proj",
    )(x2d, mix_g, w_main, cos_t, sin_t, q_norm, kv_norm, w_uq, w_ukv, gains)


CONV_ROWS = 256
CONV_CHUNK = 64


def _conv_kernel(y_ref, halo_ref, dw_ref, b_ref, lng_ref, lnb_ref, o_ref, sh_ref):
    j = pl.program_id(1)
    tm = y_ref.shape[1]
    halo = halo_ref[0].astype(F32)
    sh_ref[0, 0:CONV_HALO, :] = jnp.where(j > 0, halo, jnp.zeros_like(halo))
    sh_ref[0, CONV_HALO:, :] = y_ref[0].astype(F32)
    span = tm + CONV_HALO - SUBLANES
    for r in range(1, SUBLANES):
        sh_ref[r, 0:span, :] = sh_ref[0, r:r + span, :]
    dw = dw_ref[...]
    first = CONV_HALO - (CONV_WIDTH - 1)
    for c in range(tm // CONV_CHUNK):
        r0 = c * CONV_CHUNK
        acc = jnp.zeros((CONV_CHUNK, CONV_CH), F32) + b_ref[...]
        for k in range(CONV_WIDTH):
            phase, base = (first + k) % SUBLANES, (first + k) // SUBLANES * SUBLANES
            acc = acc + dw[k:k + 1, :] * sh_ref[phase, r0 + base:r0 + base + CONV_CHUNK, :]
        mu = jnp.mean(acc, axis=-1, keepdims=True)
        cen = acc - mu
        var = jnp.mean(cen * cen, axis=-1, keepdims=True)
        z = cen * lax.rsqrt(var + EPS) * lng_ref[...] + lnb_ref[...]
        o_ref[0, r0:r0 + CONV_CHUNK, :] = (z * jax.nn.sigmoid(z)).astype(BF16)


def _conv_branch(y, dw, b, ln_g, ln_b):
    bsz, s, ch = y.shape
    tm = min(CONV_ROWS, s)
    ratio = tm // CONV_HALO
    return pl.pallas_call(
        _conv_kernel,
        grid=(bsz, s // tm),
        in_specs=[
            pl.BlockSpec((1, tm, ch), lambda b_, j: (b_, j, 0)),
            pl.BlockSpec((1, CONV_HALO, ch), lambda b_, j: (b_, jnp.maximum(j * ratio - 1, 0), 0)),
            _resident(dw.shape), _resident(b.shape), _resident(ln_g.shape), _resident(ln_b.shape),
        ],
        out_specs=pl.BlockSpec((1, tm, ch), lambda b_, j: (b_, j, 0)),
        out_shape=jax.ShapeDtypeStruct((bsz, s, ch), BF16),
        scratch_shapes=[pltpu.VMEM((SUBLANES, tm + CONV_HALO, ch), F32)],
        compiler_params=_params("parallel", "parallel"),
        name="conformer_conv",
    )(y, y, dw, b, ln_g, ln_b)


ATT_BLOCK = 256


def _tn_dot(a, b):
    return lax.dot_general(a, b, (((0,), (0,)), ((), ())), preferred_element_type=F32)


def _sb_blocks(k_blks, v_blks, q_ts, ut, carries, outs_t, strict):
    zs = [jnp.dot(k, q, preferred_element_type=F32) for k, q in zip(k_blks, q_ts)]
    log_keeps = []
    for z in zs:
        log_keep = -(jnp.maximum(z, 0.0) + jnp.log(1.0 + jnp.exp(-jnp.abs(z))))
        if strict is not None:
            log_keep = jnp.where(strict, log_keep, 0.0)
        log_keeps.append(log_keep)
    laters = [jnp.dot(ut, lk.astype(BF16), preferred_element_type=F32) + c for lk, c in zip(log_keeps, carries)]
    new_outs = []
    for z, lk, later, v, out_t in zip(zs, log_keeps, laters, v_blks, outs_t):
        logw = z + lk + later
        if strict is not None:
            logw = jnp.where(strict, logw, MASK_VALUE)
        new_outs.append(out_t + _tn_dot(v, jnp.exp(logw).astype(BF16)))
    new_carries = [c + jnp.sum(lk, axis=0, keepdims=True) for c, lk in zip(carries, log_keeps)]
    return new_carries, new_outs


def _sb_kernel(q_ref, k_ref, v_ref, o_ref, qt_ref, ut_ref, carry_ref, acc_ref):
    i = pl.program_id(1)
    blk = q_ref.shape[1]
    key = lax.broadcasted_iota(jnp.int32, (blk, blk), 0)
    col = lax.broadcasted_iota(jnp.int32, (blk, blk), 1)
    strict = key < col
    ut_ref[...] = (col > key).astype(BF16)
    for hd in range(HEADS):
        qt_ref[hd] = q_ref[0, :, hd * HEAD_DIM:(hd + 1) * HEAD_DIM].astype(F32).T.astype(BF16)
    carry_ref[...] = jnp.zeros(carry_ref.shape, F32)
    acc_ref[...] = jnp.zeros(acc_ref.shape, F32)

    def block(kb, mask):
        start = pl.multiple_of(kb * blk, blk)
        heads = range(HEADS)
        cols = [slice(hd * HEAD_DIM, (hd + 1) * HEAD_DIM) for hd in heads]
        carries, outs_t = _sb_blocks(
            [k_ref[0, pl.ds(start, blk), c] for c in cols], [v_ref[0, pl.ds(start, blk), c] for c in cols],
            [qt_ref[hd] for hd in heads], ut_ref[...], [carry_ref[hd] for hd in heads],
            [acc_ref[hd] for hd in heads], mask)
        top = None
        for hd in heads:
            carry_ref[hd] = carries[hd]
            acc_ref[hd] = outs_t[hd]
            top = carries[hd] if top is None else jnp.maximum(top, carries[hd])
        return jnp.max(top)

    top = block(i, strict)

    def more(state):
        r, top = state
        return jnp.logical_and(r < i, top > SB_UNDERFLOW_LOG)

    def body(state):
        r, _ = state
        return r + 1, block(i - 1 - r, None)

    lax.while_loop(more, body, (jnp.int32(0), top))
    for hd in range(HEADS):
        o_ref[0, :, hd * HEAD_DIM:(hd + 1) * HEAD_DIM] = acc_ref[hd].T.astype(BF16)


def _sb_attention(q, k, v):
    bsz, s, width = q.shape
    blk = min(ATT_BLOCK, s)
    return pl.pallas_call(
        _sb_kernel,
        grid=(bsz, s // blk),
        in_specs=[
            pl.BlockSpec((1, blk, width), lambda b, i: (b, i, 0)),
            pl.BlockSpec((1, s, width), lambda b, i: (b, 0, 0)),
            pl.BlockSpec((1, s, width), lambda b, i: (b, 0, 0)),
        ],
        out_specs=pl.BlockSpec((1, blk, width), lambda b, i: (b, i, 0)),
        out_shape=jax.ShapeDtypeStruct((bsz, s, width), BF16),
        scratch_shapes=[
            pltpu.VMEM((HEADS, HEAD_DIM, blk), BF16),
            pltpu.VMEM((blk, blk), BF16),
            pltpu.VMEM((HEADS, 1, blk), F32),
            pltpu.VMEM((HEADS, HEAD_DIM, blk), F32),
        ],
        compiler_params=_params("parallel", "arbitrary"),
        name="stick_breaking_attention",
    )(q, k, v)


MLA_Q_TILE = 1024


def _mla_kernel(q_ref, k_ref, v_ref, o_ref, qt_ref, sa_ref, sb_ref, m_ref, l_ref, acc_ref):
    i = pl.program_id(2)
    tq, tk = q_ref.shape[1], sa_ref.shape[0]
    qt_ref[...] = q_ref[0].astype(F32).T.astype(BF16)
    m_ref[...] = jnp.full(m_ref.shape, MASK_VALUE, F32)
    l_ref[...] = jnp.zeros(l_ref.shape, F32)
    acc_ref[...] = jnp.zeros(acc_ref.shape, F32)
    everyone = slice(0, tq)

    def scores(kb, dst_ref, cols=everyone):
        start = kb * tk if isinstance(kb, int) else pl.multiple_of(kb * tk, tk)
        dst_ref[:, cols] = jnp.dot(k_ref[0, pl.ds(start, tk), :], qt_ref[:, cols], preferred_element_type=F32)

    def update(src_ref, kb, allowed=None, cols=everyone):
        start = pl.multiple_of(kb * tk, tk)
        s = src_ref[:, cols]
        if allowed is not None:
            s = jnp.where(allowed, s, MASK_VALUE)
        m_prev = m_ref[:, cols]
        m_new = jnp.maximum(m_prev, jnp.max(s, axis=0, keepdims=True))
        alpha = jnp.exp2(m_prev - m_new)
        p = jnp.exp2(s - m_new)
        l_ref[:, cols] = alpha * l_ref[:, cols] + jnp.sum(p, axis=0, keepdims=True)
        acc_ref[:, cols] = alpha * acc_ref[:, cols] + _tn_dot(v_ref[0, pl.ds(start, tk), :], p.astype(BF16))
        m_ref[:, cols] = m_new

    scores(0, sa_ref)

    def pair(p, carry):
        scores(2 * p + 1, sb_ref)
        update(sa_ref, 2 * p)
        scores(2 * p + 2, sa_ref)
        update(sb_ref, 2 * p + 1)
        return carry

    lax.fori_loop(0, i, pair, 0)

    early, late = slice(0, tk), slice(tk, tq)
    key_chunk = lax.broadcasted_iota(jnp.int32, (tk, tk), 0) // CHUNK
    col_chunk = lax.broadcasted_iota(jnp.int32, (tk, tk), 1) // CHUNK
    block_causal = key_chunk <= col_chunk
    scores(2 * i + 1, sb_ref, late)
    update(sa_ref, 2 * i, block_causal, early)
    update(sa_ref, 2 * i, None, late)
    update(sb_ref, 2 * i + 1, block_causal, late)
    o_ref[0] = (acc_ref[...] / l_ref[...]).T.astype(BF16)


def _mla_attention(q, k, v):
    bsz, s, _ = q.shape
    tq = min(MLA_Q_TILE, s)
    tk = tq // 2
    assert tk % CHUNK == 0 and s % tq == 0
    return pl.pallas_call(
        _mla_kernel,
        grid=(bsz, HEADS, s // tq),
        in_specs=[
            pl.BlockSpec((1, tq, MLA_QK_PAD), lambda b, h, i: (b, i, h)),
            pl.BlockSpec((1, s, MLA_QK_PAD), lambda b, h, i: (b, 0, h)),
            pl.BlockSpec((1, s, HEAD_DIM), lambda b, h, i: (b, 0, h)),
        ],
        out_specs=pl.BlockSpec((1, tq, HEAD_DIM), lambda b, h, i: (b, i, h)),
        out_shape=jax.ShapeDtypeStruct((bsz, s, HEADS * HEAD_DIM), BF16),
        scratch_shapes=[
            pltpu.VMEM((MLA_QK_PAD, tq), BF16),
            pltpu.VMEM((tk, tq), F32),
            pltpu.VMEM((tk, tq), F32),
            pltpu.VMEM((1, tq), F32),
            pltpu.VMEM((1, tq), F32),
            pltpu.VMEM((HEAD_DIM, tq), F32),
        ],
        compiler_params=_params("parallel", "parallel", "arbitrary"),
        name="chunk_causal_latent_attention",
    )(q, k, v)


def _memkv_kernel(mem_ref, g_ref, w_ref, kg_ref, k_ref, v_ref):
    m = mem_ref[0]
    mn = (m * _rms_scale(m) * g_ref[...]).astype(BF16)
    kv = jnp.dot(mn, w_ref[...], preferred_element_type=F32)
    for hd in range(HEADS):
        lo = hd * HEAD_DIM
        k = kv[:, lo:lo + HEAD_DIM]
        k_ref[0, :, lo:lo + HEAD_DIM] = (k * lax.rsqrt(_head_sumsq(k) / HEAD_DIM + EPS) * kg_ref[...]).astype(BF16)
    v_ref[0] = kv[:, HEADS * HEAD_DIM:].astype(BF16)


def _memkv(mem, g, w_kv, k_gain):
    bsz, m, d = mem.shape
    width = HEADS * HEAD_DIM
    blk = lambda w: pl.BlockSpec((1, m, w), lambda b: (b, 0, 0))
    return pl.pallas_call(
        _memkv_kernel,
        grid=(bsz,),
        in_specs=[blk(d), _resident(g.shape), _resident(w_kv.shape), _resident(k_gain.shape)],
        out_specs=[blk(width), blk(width)],
        out_shape=[jax.ShapeDtypeStruct((bsz, m, width), BF16)] * 2,
        compiler_params=_params("parallel"),
        name="memory_kv",
    )(mem, g, w_kv, k_gain)


MERGE_ROWS = 512


def _merge_kernel(x_ref, mixg_ref, wg_ref, wb_ref, wo_ref, conv_ref, sb_ref, mla_ref, memq_ref, memk_ref, memv_ref,
                  o_ref, memo_ref):
    x = x_ref[0]
    h = (x * _rms_scale(x) * mixg_ref[...]).astype(BF16)

    cols = [slice(hd * HEAD_DIM, (hd + 1) * HEAD_DIM) for hd in range(HEADS)]
    ss = [lax.dot_general(memq_ref[0, :, c], memk_ref[0, :, c], (((1,), (1,)), ((), ())),
                          preferred_element_type=F32) for c in cols]
    ps = [jnp.exp(s - jnp.max(s, axis=-1, keepdims=True)) for s in ss]
    outs = [jnp.dot(p.astype(BF16), memv_ref[0, :, c], preferred_element_type=F32) for p, c in zip(ps, cols)]
    for p, o, c in zip(ps, outs, cols):
        memo_ref[:, c] = (o / jnp.sum(p, axis=-1, keepdims=True)).astype(BF16)

    branches = (conv_ref[0], sb_ref[0], mla_ref[0], memo_ref[...])
    merged = None
    for i, br in enumerate(branches):
        gate = jax.nn.sigmoid(jnp.dot(h, wg_ref[:, i * D_MODEL:(i + 1) * D_MODEL], preferred_element_type=F32))
        term = gate * jnp.dot(br, wb_ref[i], preferred_element_type=F32)
        merged = term if merged is None else merged + term
    o_ref[0] = x + jnp.dot(merged.astype(BF16), wo_ref[...], preferred_element_type=F32)


def _merge(x, mix_g, w_gate, w_branch, w_out, conv_o, sb_o, mla_o, mem_q, mem_k, mem_v):
    bsz, s, d = x.shape
    tm = min(MERGE_ROWS, s)
    row = lambda w: pl.BlockSpec((1, tm, w), lambda b, j: (b, j, 0))
    mem = pl.BlockSpec((1,) + mem_k.shape[1:], lambda b, j: (b, 0, 0))
    return pl.pallas_call(
        _merge_kernel,
        grid=(bsz, s // tm),
        in_specs=[row(d), _resident(mix_g.shape), _resident(w_gate.shape), _resident(w_branch.shape),
                  _resident(w_out.shape), row(BRANCH_WIDTH), row(BRANCH_WIDTH), row(BRANCH_WIDTH),
                  row(BRANCH_WIDTH), mem, mem],
        out_specs=row(d),
        out_shape=jax.ShapeDtypeStruct((bsz, s, d), F32),
        scratch_shapes=[pltpu.VMEM((tm, BRANCH_WIDTH), BF16)],
        compiler_params=_params("parallel", "parallel"),
        name="gated_merge_out_proj",
    )(x, mix_g, w_gate, w_branch, w_out, conv_o, sb_o, mla_o, mem_q, mem_k, mem_v)


def _pad_lanes(a, width=LANES):
    return jnp.pad(a, [(0, 0)] * (a.ndim - 1) + [(0, width - a.shape[-1])])


def _swap_halves(a):
    half = a.shape[-1] // 2
    return jnp.concatenate([a[..., half:], a[..., :half]], axis=-1)


def _rope_tables(positions):
    inv_freq = ROPE_BASE ** (-jnp.arange(0, MLA_ROPE, 2, dtype=F32) / MLA_ROPE)
    ang = positions.astype(F32)[..., None] * inv_freq
    cos, sin = jnp.cos(ang), jnp.sin(ang)
    cos_t = _pad_lanes(jnp.concatenate([cos, cos], axis=-1))
    sin_t = _pad_lanes(jnp.concatenate([-sin, sin], axis=-1))
    return cos_t, sin_t


def _layer_params(l, w_in, sb_q_hnorm, sb_k_hnorm, mla_w_uq, mla_q_hnorm, mla_k_hnorm, mem_q_hnorm):
    w = w_in[l]
    kr = w[:, 3072:3136]
    w_main = jnp.concatenate(
        [w[:, :3072], _pad_lanes(kr), _pad_lanes(_swap_halves(kr)), w[:, 3136:3648]], axis=1).astype(BF16)
    w_gate = w[:, 3648:].astype(BF16)

    wuq = mla_w_uq[l].reshape(MLA_Q_LORA, HEADS, MLA_NOPE + MLA_ROPE)
    rot = wuq[..., MLA_NOPE:]
    w_uq = jnp.concatenate([wuq[..., :MLA_NOPE], _pad_lanes(rot), _pad_lanes(_swap_halves(rot))], axis=-1)
    w_uq = w_uq.reshape(MLA_Q_LORA, HEADS * 3 * LANES).astype(BF16)

    sb_scale = HEAD_DIM ** -0.5
    mla_scale = (MLA_NOPE + MLA_ROPE) ** -0.5 * LOG2_E
    qg, kg = mla_q_hnorm[l] * mla_scale, mla_k_hnorm[l]
    rows = [sb_q_hnorm[l] * sb_scale, sb_k_hnorm[l], mem_q_hnorm[l] * sb_scale,
            qg[:MLA_NOPE], _pad_lanes(qg[MLA_NOPE:]), _pad_lanes(_swap_halves(qg[MLA_NOPE:])),
            kg[:MLA_NOPE], _pad_lanes(kg[MLA_NOPE:]), _pad_lanes(_swap_halves(kg[MLA_NOPE:]))]
    gains = jnp.stack(rows + [jnp.zeros((LANES,), F32)] * (16 - len(rows)))
    return w_main, w_gate, w_uq, gains


def kernel(x, mem, positions, ffn1_norm, ffn1_w_in, ffn1_w_out, mix_norm, w_in, conv_dw, conv_b, conv_ln_g,
           conv_ln_b, sb_q_hnorm, sb_k_hnorm, mla_q_norm, mla_w_uq, mla_kv_norm, mla_w_ukv, mla_q_hnorm,
           mla_k_hnorm, mem_norm, mem_w_kv, mem_q_hnorm, mem_k_hnorm, w_branch, w_out, ffn2_norm, ffn2_w_in,
           ffn2_w_out):
    bsz, s, d = x.shape
    depth = w_in.shape[0]
    n = bsz * s
    cos_t, sin_t = _rope_tables(positions)
    cos_t, sin_t = cos_t.reshape(n, LANES), sin_t.reshape(n, LANES)
    row = lambda a: a.reshape(1, -1)

    for l in range(depth):
        w_main, w_gate, w_uq, gains = _layer_params(
            l, w_in, sb_q_hnorm, sb_k_hnorm, mla_w_uq, mla_q_hnorm, mla_k_hnorm, mem_q_hnorm)

        x2d = _ffn(x.reshape(n, d), row(ffn1_norm[l]), ffn1_w_in[l].astype(BF16), ffn1_w_out[l].astype(BF16))

        y, sbq, sbk, sbv, mq, mk, mv, memq = _proj(
            x2d, row(mix_norm[l]), w_main, cos_t, sin_t, row(mla_q_norm[l]), row(mla_kv_norm[l]), w_uq,
            mla_w_ukv[l].astype(BF16), gains)
        seq = lambda a: a.reshape(bsz, s, a.shape[-1])

        conv_o = _conv_branch(seq(y), conv_dw[l], row(conv_b[l]), row(conv_ln_g[l]), row(conv_ln_b[l]))
        sb_o = _sb_attention(seq(sbq), seq(sbk), seq(sbv))
        mla_o = _mla_attention(seq(mq), seq(mk), seq(mv))
        mem_k, mem_v = _memkv(mem, row(mem_norm[l]), mem_w_kv[l].astype(BF16), row(mem_k_hnorm[l]))

        x = _merge(seq(x2d), row(mix_norm[l]), w_gate, w_branch[l].astype(BF16), w_out[l].astype(BF16),
                   conv_o, sb_o, mla_o, seq(memq), mem_k, mem_v)

        x = _ffn(x.reshape(n, d), row(ffn2_norm[l]), ffn2_w_in[l].astype(BF16),
                 ffn2_w_out[l].astype(BF16)).reshape(bsz, s, d)
    return x
```

```python
import functools

import jax
import jax.numpy as jnp
from jax import lax
from jax.experimental import pallas as pl
from jax.experimental.pallas import tpu as pltpu

F32 = jnp.float32
BF16 = jnp.bfloat16

D_MODEL = 1024
FFN_HIDDEN = 2048
CHUNK = 64
CONV_CH = 512
CONV_WIDTH = 31
HEADS = 4
HEAD_DIM = 128
MLA_NOPE = 128
MLA_ROPE = 64
MLA_QK_PAD = 256
MLA_Q_LORA = 256
MLA_KV_LORA = 256
N_BRANCH = 4
BRANCH_WIDTH = 512
ROPE_BASE = 10000.0
EPS = 1e-6
MASK_VALUE = -1e30
LOG2_E = 1.4426950408889634
SB_UNDERFLOW_LOG = -110.0

LANES = 128
SUBLANES = 8
CONV_HALO = 32
V7X_VMEM_LIMIT_BYTES = 56 * 1024 * 1024

_C_CONV = 0
_C_SBQ = 1024
_C_SBK = 1536
_C_SBV = 2048
_C_QLAT = 2560
_C_KVLAT = 2816
_C_KR = 3072
_C_KRSW = 3200
_C_MEMQ = 3328
_C_END = 3840


def _params(*sem):
    return pltpu.CompilerParams(dimension_semantics=sem, vmem_limit_bytes=V7X_VMEM_LIMIT_BYTES)


def _resident(shape):
    nd = len(shape)
    return pl.BlockSpec(shape, lambda *_: (0,) * nd)


def _rms_scale(x):
    return lax.rsqrt(jnp.mean(x * x, axis=-1, keepdims=True) + EPS)


FFN_ROWS = 1024
FFN_COLS = 512


def _ffn_kernel(x_ref, g_ref, win_ref, wout_ref, o_ref, act_ref):
    x = x_ref[...]
    h = (x * _rms_scale(x) * g_ref[...]).astype(BF16)
    for c in range(FFN_HIDDEN // FFN_COLS):
        lo = c * FFN_COLS
        gate = jnp.dot(h, win_ref[:, lo:lo + FFN_COLS], preferred_element_type=F32)
        up = jnp.dot(h, win_ref[:, FFN_HIDDEN + lo:FFN_HIDDEN + lo + FFN_COLS], preferred_element_type=F32)
        act_ref[:, lo:lo + FFN_COLS] = (gate * jax.nn.sigmoid(gate) * up).astype(BF16)
    y = jnp.dot(act_ref[...], wout_ref[...], preferred_element_type=F32)
    o_ref[...] = x + 0.5 * y


def _ffn(x2d, g, w_in, w_out):
    n, d = x2d.shape
    tm = min(FFN_ROWS, n)
    return pl.pallas_call(
        _ffn_kernel,
        grid=(n // tm,),
        in_specs=[
            pl.BlockSpec((tm, d), lambda i: (i, 0)),
            _resident(g.shape),
            _resident(w_in.shape),
            _resident(w_out.shape),
        ],
        out_specs=pl.BlockSpec((tm, d), lambda i: (i, 0)),
        out_shape=jax.ShapeDtypeStruct((n, d), F32),
        scratch_shapes=[pltpu.VMEM((tm, FFN_HIDDEN), BF16)],
        compiler_params=_params("parallel"),
        name="ffn_half_step",
    )(x2d, g, w_in, w_out)


PROJ_ROWS = 512


def _head_sumsq(x):
    return jnp.sum(x * x, axis=-1, keepdims=True)


def _proj_kernel(x_ref, mixg_ref, w_ref, cos_ref, sin_ref, qn_ref, kvn_ref, wuq_ref, wukv_ref, gains_ref,
                 y_ref, sbq_ref, sbk_ref, sbv_ref, mq_ref, mk_ref, mv_ref, memq_ref):
    x = x_ref[...]
    h = (x * _rms_scale(x) * mixg_ref[...]).astype(BF16)
    u = jnp.dot(h, w_ref[...], preferred_element_type=F32)
    gains = gains_ref[...]
    sb_qg, sb_kg, mem_qg = gains[0:1], gains[1:2], gains[2:3]
    q_ga, q_gb, q_gbsw = gains[3:4], gains[4:5], gains[5:6]
    k_ga, k_gb, k_gbsw = gains[6:7], gains[7:8], gains[8:9]
    cos_t = cos_ref[...]
    sin_t = sin_ref[...]

    y_ref[...] = (u[:, _C_CONV:_C_CONV + CONV_CH]
                  * jax.nn.sigmoid(u[:, _C_CONV + CONV_CH:_C_CONV + 2 * CONV_CH])).astype(BF16)

    for hd in range(HEADS):
        lo = hd * HEAD_DIM
        q = u[:, _C_SBQ + lo:_C_SBQ + lo + HEAD_DIM]
        k = u[:, _C_SBK + lo:_C_SBK + lo + HEAD_DIM]
        m = u[:, _C_MEMQ + lo:_C_MEMQ + lo + HEAD_DIM]
        sbq_ref[:, lo:lo + HEAD_DIM] = (q * lax.rsqrt(_head_sumsq(q) / HEAD_DIM + EPS) * sb_qg).astype(BF16)
        sbk_ref[:, lo:lo + HEAD_DIM] = (k * lax.rsqrt(_head_sumsq(k) / HEAD_DIM + EPS) * sb_kg).astype(BF16)
        memq_ref[:, lo:lo + HEAD_DIM] = (m * lax.rsqrt(_head_sumsq(m) / HEAD_DIM + EPS) * mem_qg).astype(BF16)
    sbv_ref[...] = u[:, _C_SBV:_C_SBV + HEADS * HEAD_DIM].astype(BF16)

    q_lat = u[:, _C_QLAT:_C_QLAT + MLA_Q_LORA]
    kv_lat = u[:, _C_KVLAT:_C_KVLAT + MLA_KV_LORA]
    q_lat_n = (q_lat * _rms_scale(q_lat) * qn_ref[...]).astype(BF16)
    kv_lat_n = (kv_lat * _rms_scale(kv_lat) * kvn_ref[...]).astype(BF16)
    q_up = jnp.dot(q_lat_n, wuq_ref[...], preferred_element_type=F32)
    kv_up = jnp.dot(kv_lat_n, wukv_ref[...], preferred_element_type=F32)
    kr = u[:, _C_KR:_C_KR + LANES]
    kr_sw = u[:, _C_KRSW:_C_KRSW + LANES]
    kr_ss = _head_sumsq(kr)
    qk_dim = MLA_NOPE + MLA_ROPE
    for hd in range(HEADS):
        qa = q_up[:, hd * 384:hd * 384 + LANES]
        qb = q_up[:, hd * 384 + LANES:hd * 384 + 2 * LANES]
        qbsw = q_up[:, hd * 384 + 2 * LANES:hd * 384 + 3 * LANES]
        qr = lax.rsqrt((_head_sumsq(qa) + _head_sumsq(qb)) / qk_dim + EPS)
        mq_ref[:, hd * MLA_QK_PAD:hd * MLA_QK_PAD + LANES] = (qa * qr * q_ga).astype(BF16)
        mq_ref[:, hd * MLA_QK_PAD + LANES:(hd + 1) * MLA_QK_PAD] = (
            (qb * q_gb * cos_t + qbsw * q_gbsw * sin_t) * qr).astype(BF16)
        ka = kv_up[:, hd * 256:hd * 256 + LANES]
        kr_inv = lax.rsqrt((_head_sumsq(ka) + kr_ss) / qk_dim + EPS)
        mk_ref[:, hd * MLA_QK_PAD:hd * MLA_QK_PAD + LANES] = (ka * kr_inv * k_ga).astype(BF16)
        mk_ref[:, hd * MLA_QK_PAD + LANES:(hd + 1) * MLA_QK_PAD] = (
            (kr * k_gb * cos_t + kr_sw * k_gbsw * sin_t) * kr_inv).astype(BF16)
        mv_ref[:, hd * HEAD_DIM:(hd + 1) * HEAD_DIM] = kv_up[:, hd * 256 + LANES:(hd + 1) * 256].astype(BF16)


def _proj(x2d, mix_g, w_main, cos_t, sin_t, q_norm, kv_norm, w_uq, w_ukv, gains):
    n, d = x2d.shape
    tm = min(PROJ_ROWS, n)
    row = lambda w: pl.BlockSpec((tm, w), lambda i: (i, 0))
    widths = (CONV_CH, 512, 512, 512, HEADS * MLA_QK_PAD, HEADS * MLA_QK_PAD, 512, 512)
    return pl.pallas_call(
        _proj_kernel,
        grid=(n // tm,),
        in_specs=[row(d), _resident(mix_g.shape), _resident(w_main.shape), row(LANES), row(LANES),
                  _resident(q_norm.shape), _resident(kv_norm.shape), _resident(w_uq.shape),
                  _resident(w_ukv.shape), _resident(gains.shape)],
        out_specs=[row(w) for w in widths],
        out_shape=[jax.ShapeDtypeStruct((n, w), BF16) for w in widths],
        compiler_params=_params("parallel"),
        name="mix_in_proj",
    )(x2d, mix_g, w_main, cos_t, sin_t, q_norm, kv_norm, w_uq, w_ukv, gains)


CONV_ROWS = 256
CONV_CHUNK = 64


def _conv_kernel(y_ref, halo_ref, dw_ref, b_ref, lng_ref, lnb_ref, o_ref, sh_ref):
    j = pl.program_id(1)
    tm = y_ref.shape[1]
    halo = halo_ref[0].astype(F32)
    sh_ref[0, 0:CONV_HALO, :] = jnp.where(j > 0, halo, jnp.zeros_like(halo))
    sh_ref[0, CONV_HALO:, :] = y_ref[0].astype(F32)
    span = tm + CONV_HALO - SUBLANES
    for r in range(1, SUBLANES):
        sh_ref[r, 0:span, :] = sh_ref[0, r:r + span, :]
    dw = dw_ref[...]
    first = CONV_HALO - (CONV_WIDTH - 1)
    for c in range(tm // CONV_CHUNK):
        r0 = c * CONV_CHUNK
        acc = jnp.zeros((CONV_CHUNK, CONV_CH), F32) + b_ref[...]
        for k in range(CONV_WIDTH):
            phase, base = (first + k) % SUBLANES, (first + k) // SUBLANES * SUBLANES
            acc = acc + dw[k:k + 1, :] * sh_ref[phase, r0 + base:r0 + base + CONV_CHUNK, :]
        mu = jnp.mean(acc, axis=-1, keepdims=True)
        cen = acc - mu
        var = jnp.mean(cen * cen, axis=-1, keepdims=True)
        z = cen * lax.rsqrt(var + EPS) * lng_ref[...] + lnb_ref[...]
        o_ref[0, r0:r0 + CONV_CHUNK, :] = (z * jax.nn.sigmoid(z)).astype(BF16)


def _conv_branch(y, dw, b, ln_g, ln_b):
    bsz, s, ch = y.shape
    tm = min(CONV_ROWS, s)
    ratio = tm // CONV_HALO
    return pl.pallas_call(
        _conv_kernel,
        grid=(bsz, s // tm),
        in_specs=[
            pl.BlockSpec((1, tm, ch), lambda b_, j: (b_, j, 0)),
            pl.BlockSpec((1, CONV_HALO, ch), lambda b_, j: (b_, jnp.maximum(j * ratio - 1, 0), 0)),
            _resident(dw.shape), _resident(b.shape), _resident(ln_g.shape), _resident(ln_b.shape),
        ],
        out_specs=pl.BlockSpec((1, tm, ch), lambda b_, j: (b_, j, 0)),
        out_shape=jax.ShapeDtypeStruct((bsz, s, ch), BF16),
        scratch_shapes=[pltpu.VMEM((SUBLANES, tm + CONV_HALO, ch), F32)],
        compiler_params=_params("parallel", "parallel"),
        name="conformer_conv",
    )(y, y, dw, b, ln_g, ln_b)


ATT_BLOCK = 256


def _tn_dot(a, b):
    return lax.dot_general(a, b, (((0,), (0,)), ((), ())), preferred_element_type=F32)


def _sb_blocks(k_blks, v_blks, q_ts, ut, carries, outs_t, strict):
    zs = [jnp.dot(k, q, preferred_element_type=F32) for k, q in zip(k_blks, q_ts)]
    log_keeps = []
    for z in zs:
        log_keep = -(jnp.maximum(z, 0.0) + jnp.log(1.0 + jnp.exp(-jnp.abs(z))))
        if strict is not None:
            log_keep = jnp.where(strict, log_keep, 0.0)
        log_keeps.append(log_keep)
    laters = [jnp.dot(ut, lk.astype(BF16), preferred_element_type=F32) + c for lk, c in zip(log_keeps, carries)]
    new_outs = []
    for z, lk, later, v, out_t in zip(zs, log_keeps, laters, v_blks, outs_t):
        logw = z + lk + later
        if strict is not None:
            logw = jnp.where(strict, logw, MASK_VALUE)
        new_outs.append(out_t + _tn_dot(v, jnp.exp(logw).astype(BF16)))
    new_carries = [c + jnp.sum(lk, axis=0, keepdims=True) for c, lk in zip(carries, log_keeps)]
    return new_carries, new_outs


def _sb_kernel(q_ref, k_ref, v_ref, o_ref, qt_ref, ut_ref, carry_ref, acc_ref):
    i = pl.program_id(1)
    blk = q_ref.shape[1]
    key = lax.broadcasted_iota(jnp.int32, (blk, blk), 0)
    col = lax.broadcasted_iota(jnp.int32, (blk, blk), 1)
    strict = key < col
    ut_ref[...] = (col > key).astype(BF16)
    for hd in range(HEADS):
        qt_ref[hd] = q_ref[0, :, hd * HEAD_DIM:(hd + 1) * HEAD_DIM].astype(F32).T.astype(BF16)
    carry_ref[...] = jnp.zeros(carry_ref.shape, F32)
    acc_ref[...] = jnp.zeros(acc_ref.shape, F32)

    def block(kb, mask):
        start = pl.multiple_of(kb * blk, blk)
        heads = range(HEADS)
        cols = [slice(hd * HEAD_DIM, (hd + 1) * HEAD_DIM) for hd in heads]
        carries, outs_t = _sb_blocks(
            [k_ref[0, pl.ds(start, blk), c] for c in cols], [v_ref[0, pl.ds(start, blk), c] for c in cols],
            [qt_ref[hd] for hd in heads], ut_ref[...], [carry_ref[hd] for hd in heads],
            [acc_ref[hd] for hd in heads], mask)
        top = None
        for hd in heads:
            carry_ref[hd] = carries[hd]
            acc_ref[hd] = outs_t[hd]
            top = carries[hd] if top is None else jnp.maximum(top, carries[hd])
        return jnp.max(top)

    top = block(i, strict)

    def more(state):
        r, top = state
        return jnp.logical_and(r < i, top > SB_UNDERFLOW_LOG)

    def body(state):
        r, _ = state
        return r + 1, block(i - 1 - r, None)

    lax.while_loop(more, body, (jnp.int32(0), top))
    for hd in range(HEADS):
        o_ref[0, :, hd * HEAD_DIM:(hd + 1) * HEAD_DIM] = acc_ref[hd].T.astype(BF16)


def _sb_attention(q, k, v):
    bsz, s, width = q.shape
    blk = min(ATT_BLOCK, s)
    return pl.pallas_call(
        _sb_kernel,
        grid=(bsz, s // blk),
        in_specs=[
            pl.BlockSpec((1, blk, width), lambda b, i: (b, i, 0)),
            pl.BlockSpec((1, s, width), lambda b, i: (b, 0, 0)),
            pl.BlockSpec((1, s, width), lambda b, i: (b, 0, 0)),
        ],
        out_specs=pl.BlockSpec((1, blk, width), lambda b, i: (b, i, 0)),
        out_shape=jax.ShapeDtypeStruct((bsz, s, width), BF16),
        scratch_shapes=[
            pltpu.VMEM((HEADS, HEAD_DIM, blk), BF16),
            pltpu.VMEM((blk, blk), BF16),
            pltpu.VMEM((HEADS, 1, blk), F32),
            pltpu.VMEM((HEADS, HEAD_DIM, blk), F32),
        ],
        compiler_params=_params("parallel", "arbitrary"),
        name="stick_breaking_attention",
    )(q, k, v)


MLA_Q_TILE = 1024


def _mla_kernel(q_ref, k_ref, v_ref, o_ref, qt_ref, sa_ref, sb_ref, m_ref, l_ref, acc_ref):
    i = pl.program_id(2)
    tq, tk = q_ref.shape[1], sa_ref.shape[0]
    qt_ref[...] = q_ref[0].astype(F32).T.astype(BF16)
    m_ref[...] = jnp.full(m_ref.shape, MASK_VALUE, F32)
    l_ref[...] = jnp.zeros(l_ref.shape, F32)
    acc_ref[...] = jnp.zeros(acc_ref.shape, F32)
    everyone = slice(0, tq)

    def scores(kb, dst_ref, cols=everyone):
        start = kb * tk if isinstance(kb, int) else pl.multiple_of(kb * tk, tk)
        dst_ref[:, cols] = jnp.dot(k_ref[0, pl.ds(start, tk), :], qt_ref[:, cols], preferred_element_type=F32)

    def update(src_ref, kb, allowed=None, cols=everyone):
        start = pl.multiple_of(kb * tk, tk)
        s = src_ref[:, cols]
        if allowed is not None:
            s = jnp.where(allowed, s, MASK_VALUE)
        m_prev = m_ref[:, cols]
        m_new = jnp.maximum(m_prev, jnp.max(s, axis=0, keepdims=True))
        alpha = jnp.exp2(m_prev - m_new)
        p = jnp.exp2(s - m_new)
        l_ref[:, cols] = alpha * l_ref[:, cols] + jnp.sum(p, axis=0, keepdims=True)
        acc_ref[:, cols] = alpha * acc_ref[:, cols] + _tn_dot(v_ref[0, pl.ds(start, tk), :], p.astype(BF16))
        m_ref[:, cols] = m_new

    scores(0, sa_ref)

    def pair(p):
        scores(2 * p + 1, sb_ref)
        update(sa_ref, 2 * p)
        scores(2 * p + 2, sa_ref)
        update(sb_ref, 2 * p + 1)

    odd = jnp.bitwise_and(i, 1)
    pl.when(odd == 1)(lambda: pair(0))

    def two_pairs(t, carry):
        pair(odd + 2 * t)
        pair(odd + 2 * t + 1)
        return carry

    lax.fori_loop(0, lax.shift_right_logical(i, 1), two_pairs, 0)

    early, late = slice(0, tk), slice(tk, tq)
    key_chunk = lax.broadcasted_iota(jnp.int32, (tk, tk), 0) // CHUNK
    col_chunk = lax.broadcasted_iota(jnp.int32, (tk, tk), 1) // CHUNK
    block_causal = key_chunk <= col_chunk
    scores(2 * i + 1, sb_ref, late)
    update(sa_ref, 2 * i, block_causal, early)
    update(sa_ref, 2 * i, None, late)
    update(sb_ref, 2 * i + 1, block_causal, late)
    o_ref[0] = (acc_ref[...] / l_ref[...]).T.astype(BF16)


def _mla_attention(q, k, v):
    bsz, s, _ = q.shape
    tq = min(MLA_Q_TILE, s)
    tk = tq // 2
    assert tk % CHUNK == 0 and s % tq == 0
    return pl.pallas_call(
        _mla_kernel,
        grid=(bsz, HEADS, s // tq),
        in_specs=[
            pl.BlockSpec((1, tq, MLA_QK_PAD), lambda b, h, i: (b, i, h)),
            pl.BlockSpec((1, s, MLA_QK_PAD), lambda b, h, i: (b, 0, h)),
            pl.BlockSpec((1, s, HEAD_DIM), lambda b, h, i: (b, 0, h)),
        ],
        out_specs=pl.BlockSpec((1, tq, HEAD_DIM), lambda b, h, i: (b, i, h)),
        out_shape=jax.ShapeDtypeStruct((bsz, s, HEADS * HEAD_DIM), BF16),
        scratch_shapes=[
            pltpu.VMEM((MLA_QK_PAD, tq), BF16),
            pltpu.VMEM((tk, tq), F32),
            pltpu.VMEM((tk, tq), F32),
            pltpu.VMEM((1, tq), F32),
            pltpu.VMEM((1, tq), F32),
            pltpu.VMEM((HEAD_DIM, tq), F32),
        ],
        compiler_params=_params("parallel", "parallel", "arbitrary"),
        name="chunk_causal_latent_attention",
    )(q, k, v)


def _memkv_kernel(mem_ref, g_ref, w_ref, kg_ref, k_ref, v_ref):
    m = mem_ref[0]
    mn = (m * _rms_scale(m) * g_ref[...]).astype(BF16)
    kv = jnp.dot(mn, w_ref[...], preferred_element_type=F32)
    for hd in range(HEADS):
        lo = hd * HEAD_DIM
        k = kv[:, lo:lo + HEAD_DIM]
        k_ref[0, :, lo:lo + HEAD_DIM] = (k * lax.rsqrt(_head_sumsq(k) / HEAD_DIM + EPS) * kg_ref[...]).astype(BF16)
    v_ref[0] = kv[:, HEADS * HEAD_DIM:].astype(BF16)


def _memkv(mem, g, w_kv, k_gain):
    bsz, m, d = mem.shape
    width = HEADS * HEAD_DIM
    blk = lambda w: pl.BlockSpec((1, m, w), lambda b: (b, 0, 0))
    return pl.pallas_call(
        _memkv_kernel,
        grid=(bsz,),
        in_specs=[blk(d), _resident(g.shape), _resident(w_kv.shape), _resident(k_gain.shape)],
        out_specs=[blk(width), blk(width)],
        out_shape=[jax.ShapeDtypeStruct((bsz, m, width), BF16)] * 2,
        compiler_params=_params("parallel"),
        name="memory_kv",
    )(mem, g, w_kv, k_gain)


MERGE_ROWS = 512


def _merge_kernel(x_ref, mixg_ref, wg_ref, wb_ref, wo_ref, conv_ref, sb_ref, mla_ref, memq_ref, memk_ref, memv_ref,
                  o_ref, memo_ref):
    x = x_ref[0]
    h = (x * _rms_scale(x) * mixg_ref[...]).astype(BF16)

    cols = [slice(hd * HEAD_DIM, (hd + 1) * HEAD_DIM) for hd in range(HEADS)]
    ss = [lax.dot_general(memq_ref[0, :, c], memk_ref[0, :, c], (((1,), (1,)), ((), ())),
                          preferred_element_type=F32) for c in cols]
    ps = [jnp.exp(s - jnp.max(s, axis=-1, keepdims=True)) for s in ss]
    outs = [jnp.dot(p.astype(BF16), memv_ref[0, :, c], preferred_element_type=F32) for p, c in zip(ps, cols)]
    for p, o, c in zip(ps, outs, cols):
        memo_ref[:, c] = (o / jnp.sum(p, axis=-1, keepdims=True)).astype(BF16)

    branches = (conv_ref[0], sb_ref[0], mla_ref[0], memo_ref[...])
    merged = None
    for i, br in enumerate(branches):
        gate = jax.nn.sigmoid(jnp.dot(h, wg_ref[:, i * D_MODEL:(i + 1) * D_MODEL], preferred_element_type=F32))
        term = gate * jnp.dot(br, wb_ref[i], preferred_element_type=F32)
        merged = term if merged is None else merged + term
    o_ref[0] = x + jnp.dot(merged.astype(BF16), wo_ref[...], preferred_element_type=F32)


def _merge(x, mix_g, w_gate, w_branch, w_out, conv_o, sb_o, mla_o, mem_q, mem_k, mem_v):
    bsz, s, d = x.shape
    tm = min(MERGE_ROWS, s)
    row = lambda w: pl.BlockSpec((1, tm, w), lambda b, j: (b, j, 0))
    mem = pl.BlockSpec((1,) + mem_k.shape[1:], lambda b, j: (b, 0, 0))
    return pl.pallas_call(
        _merge_kernel,
        grid=(bsz, s // tm),
        in_specs=[row(d), _resident(mix_g.shape), _resident(w_gate.shape), _resident(w_branch.shape),
                  _resident(w_out.shape), row(BRANCH_WIDTH), row(BRANCH_WIDTH), row(BRANCH_WIDTH),
                  row(BRANCH_WIDTH), mem, mem],
        out_specs=row(d),
        out_shape=jax.ShapeDtypeStruct((bsz, s, d), F32),
        scratch_shapes=[pltpu.VMEM((tm, BRANCH_WIDTH), BF16)],
        compiler_params=_params("parallel", "parallel"),
        name="gated_merge_out_proj",
    )(x, mix_g, w_gate, w_branch, w_out, conv_o, sb_o, mla_o, mem_q, mem_k, mem_v)


def _pad_lanes(a, width=LANES):
    return jnp.pad(a, [(0, 0)] * (a.ndim - 1) + [(0, width - a.shape[-1])])


def _swap_halves(a):
    half = a.shape[-1] // 2
    return jnp.concatenate([a[..., half:], a[..., :half]], axis=-1)


def _rope_tables(positions):
    inv_freq = ROPE_BASE ** (-jnp.arange(0, MLA_ROPE, 2, dtype=F32) / MLA_ROPE)
    ang = positions.astype(F32)[..., None] * inv_freq
    cos, sin = jnp.cos(ang), jnp.sin(ang)
    cos_t = _pad_lanes(jnp.concatenate([cos, cos], axis=-1))
    sin_t = _pad_lanes(jnp.concatenate([-sin, sin], axis=-1))
    return cos_t, sin_t


def _layer_params(l, w_in, sb_q_hnorm, sb_k_hnorm, mla_w_uq, mla_q_hnorm, mla_k_hnorm, mem_q_hnorm):
    w = w_in[l]
    kr = w[:, 3072:3136]
    w_main = jnp.concatenate(
        [w[:, :3072], _pad_lanes(kr), _pad_lanes(_swap_halves(kr)), w[:, 3136:3648]], axis=1).astype(BF16)
    w_gate = w[:, 3648:].astype(BF16)

    wuq = mla_w_uq[l].reshape(MLA_Q_LORA, HEADS, MLA_NOPE + MLA_ROPE)
    rot = wuq[..., MLA_NOPE:]
    w_uq = jnp.concatenate([wuq[..., :MLA_NOPE], _pad_lanes(rot), _pad_lanes(_swap_halves(rot))], axis=-1)
    w_uq = w_uq.reshape(MLA_Q_LORA, HEADS * 3 * LANES).astype(BF16)

    sb_scale = HEAD_DIM ** -0.5
    mla_scale = (MLA_NOPE + MLA_ROPE) ** -0.5 * LOG2_E
    qg, kg = mla_q_hnorm[l] * mla_scale, mla_k_hnorm[l]
    rows = [sb_q_hnorm[l] * sb_scale, sb_k_hnorm[l], mem_q_hnorm[l] * sb_scale,
            qg[:MLA_NOPE], _pad_lanes(qg[MLA_NOPE:]), _pad_lanes(_swap_halves(qg[MLA_NOPE:])),
            kg[:MLA_NOPE], _pad_lanes(kg[MLA_NOPE:]), _pad_lanes(_swap_halves(kg[MLA_NOPE:]))]
    gains = jnp.stack(rows + [jnp.zeros((LANES,), F32)] * (16 - len(rows)))
    return w_main, w_gate, w_uq, gains


def kernel(x, mem, positions, ffn1_norm, ffn1_w_in, ffn1_w_out, mix_norm, w_in, conv_dw, conv_b, conv_ln_g,
           conv_ln_b, sb_q_hnorm, sb_k_hnorm, mla_q_norm, mla_w_uq, mla_kv_norm, mla_w_ukv, mla_q_hnorm,
           mla_k_hnorm, mem_norm, mem_w_kv, mem_q_hnorm, mem_k_hnorm, w_branch, w_out, ffn2_norm, ffn2_w_in,
           ffn2_w_out):
    bsz, s, d = x.shape
    depth = w_in.shape[0]
    n = bsz * s
    cos_t, sin_t = _rope_tables(positions)
    cos_t, sin_t = cos_t.reshape(n, LANES), sin_t.reshape(n, LANES)
    row = lambda a: a.reshape(1, -1)

    for l in range(depth):
        w_main, w_gate, w_uq, gains = _layer_params(
            l, w_in, sb_q_hnorm, sb_k_hnorm, mla_w_uq, mla_q_hnorm, mla_k_hnorm, mem_q_hnorm)

        x2d = _ffn(x.reshape(n, d), row(ffn1_norm[l]), ffn1_w_in[l].astype(BF16), ffn1_w_out[l].astype(BF16))

        y, sbq, sbk, sbv, mq, mk, mv, memq = _proj(
            x2d, row(mix_norm[l]), w_main, cos_t, sin_t, row(mla_q_norm[l]), row(mla_kv_norm[l]), w_uq,
            mla_w_ukv[l].astype(BF16), gains)
        seq = lambda a: a.reshape(bsz, s, a.shape[-1])

        conv_o = _conv_branch(seq(y), conv_dw[l], row(conv_b[l]), row(conv_ln_g[l]), row(conv_ln_b[l]))
        sb_o = _sb_attention(seq(sbq), seq(sbk), seq(sbv))
        mla_o = _mla_attention(seq(mq), seq(mk), seq(mv))
        mem_k, mem_v = _memkv(mem, row(mem_norm[l]), mem_w_kv[l].astype(BF16), row(mem_k_hnorm[l]))

        x = _merge(seq(x2d), row(mix_norm[l]), w_gate, w_branch[l].astype(BF16), w_out[l].astype(BF16),
                   conv_o, sb_o, mla_o, seq(memq), mem_k, mem_v)

        x = _ffn(x.reshape(n, d), row(ffn2_norm[l]), ffn2_w_in[l].astype(BF16),
                 ffn2_w_out[l].astype(BF16)).reshape(bsz, s, d)
    return x
```

```python
import functools

import jax
import jax.numpy as jnp
from jax import lax
from jax.experimental import pallas as pl
from jax.experimental.pallas import tpu as pltpu

F32 = jnp.float32
BF16 = jnp.bfloat16

D_MODEL = 1024
FFN_HIDDEN = 2048
CHUNK = 64
CONV_CH = 512
CONV_WIDTH = 31
HEADS = 4
HEAD_DIM = 128
MLA_NOPE = 128
MLA_ROPE = 64
MLA_QK_PAD = 256
MLA_Q_LORA = 256
MLA_KV_LORA = 256
N_BRANCH = 4
BRANCH_WIDTH = 512
ROPE_BASE = 10000.0
EPS = 1e-6
MASK_VALUE = -1e30
LOG2_E = 1.4426950408889634
SB_UNDERFLOW_LOG = -110.0

LANES = 128
SUBLANES = 8
CONV_HALO = 32
V7X_VMEM_LIMIT_BYTES = 56 * 1024 * 1024

_C_CONV = 0
_C_SBQ = 1024
_C_SBK = 1536
_C_SBV = 2048
_C_QLAT = 2560
_C_KVLAT = 2816
_C_KR = 3072
_C_KRSW = 3200
_C_MEMQ = 3328
_C_END = 3840


def _params(*sem):
    return pltpu.CompilerParams(dimension_semantics=sem, vmem_limit_bytes=V7X_VMEM_LIMIT_BYTES)


def _resident(shape):
    nd = len(shape)
    return pl.BlockSpec(shape, lambda *_: (0,) * nd)


def _rms_scale(x):
    return lax.rsqrt(jnp.mean(x * x, axis=-1, keepdims=True) + EPS)


FFN_ROWS = 1024
FFN_COLS = 512


def _ffn_kernel(x_ref, g_ref, win_ref, wout_ref, o_ref, act_ref):
    x = x_ref[...]
    h = (x * _rms_scale(x) * g_ref[...]).astype(BF16)
    for c in range(FFN_HIDDEN // FFN_COLS):
        lo = c * FFN_COLS
        gate = jnp.dot(h, win_ref[:, lo:lo + FFN_COLS], preferred_element_type=F32)
        up = jnp.dot(h, win_ref[:, FFN_HIDDEN + lo:FFN_HIDDEN + lo + FFN_COLS], preferred_element_type=F32)
        act_ref[:, lo:lo + FFN_COLS] = (gate * jax.nn.sigmoid(gate) * up).astype(BF16)
    y = jnp.dot(act_ref[...], wout_ref[...], preferred_element_type=F32)
    o_ref[...] = x + 0.5 * y


def _ffn(x2d, g, w_in, w_out):
    n, d = x2d.shape
    tm = min(FFN_ROWS, n)
    return pl.pallas_call(
        _ffn_kernel,
        grid=(n // tm,),
        in_specs=[
            pl.BlockSpec((tm, d), lambda i: (i, 0)),
            _resident(g.shape),
            _resident(w_in.shape),
            _resident(w_out.shape),
        ],
        out_specs=pl.BlockSpec((tm, d), lambda i: (i, 0)),
        out_shape=jax.ShapeDtypeStruct((n, d), F32),
        scratch_shapes=[pltpu.VMEM((tm, FFN_HIDDEN), BF16)],
        compiler_params=_params("parallel"),
        name="ffn_half_step",
    )(x2d, g, w_in, w_out)


PROJ_ROWS = 512


def _head_sumsq(x):
    return jnp.sum(x * x, axis=-1, keepdims=True)


def _proj_kernel(x_ref, mixg_ref, w_ref, cos_ref, sin_ref, qn_ref, kvn_ref, wuq_ref, wukv_ref, gains_ref,
                 y_ref, sbq_ref, sbk_ref, sbv_ref, mq_ref, mk_ref, mv_ref, memq_ref):
    x = x_ref[...]
    h = (x * _rms_scale(x) * mixg_ref[...]).astype(BF16)
    u = jnp.dot(h, w_ref[...], preferred_element_type=F32)
    gains = gains_ref[...]
    sb_qg, sb_kg, mem_qg = gains[0:1], gains[1:2], gains[2:3]
    q_ga, q_gb, q_gbsw = gains[3:4], gains[4:5], gains[5:6]
    k_ga, k_gb, k_gbsw = gains[6:7], gains[7:8], gains[8:9]
    cos_t = cos_ref[...]
    sin_t = sin_ref[...]

    y_ref[...] = (u[:, _C_CONV:_C_CONV + CONV_CH]
                  * jax.nn.sigmoid(u[:, _C_CONV + CONV_CH:_C_CONV + 2 * CONV_CH])).astype(BF16)

    for hd in range(HEADS):
        lo = hd * HEAD_DIM
        q = u[:, _C_SBQ + lo:_C_SBQ + lo + HEAD_DIM]
        k = u[:, _C_SBK + lo:_C_SBK + lo + HEAD_DIM]
        m = u[:, _C_MEMQ + lo:_C_MEMQ + lo + HEAD_DIM]
        sbq_ref[:, lo:lo + HEAD_DIM] = (q * lax.rsqrt(_head_sumsq(q) / HEAD_DIM + EPS) * sb_qg).astype(BF16)
        sbk_ref[:, lo:lo + HEAD_DIM] = (k * lax.rsqrt(_head_sumsq(k) / HEAD_DIM + EPS) * sb_kg).astype(BF16)
        memq_ref[:, lo:lo + HEAD_DIM] = (m * lax.rsqrt(_head_sumsq(m) / HEAD_DIM + EPS) * mem_qg).astype(BF16)
    sbv_ref[...] = u[:, _C_SBV:_C_SBV + HEADS * HEAD_DIM].astype(BF16)

    q_lat = u[:, _C_QLAT:_C_QLAT + MLA_Q_LORA]
    kv_lat = u[:, _C_KVLAT:_C_KVLAT + MLA_KV_LORA]
    q_lat_n = (q_lat * _rms_scale(q_lat) * qn_ref[...]).astype(BF16)
    kv_lat_n = (kv_lat * _rms_scale(kv_lat) * kvn_ref[...]).astype(BF16)
    q_up = jnp.dot(q_lat_n, wuq_ref[...], preferred_element_type=F32)
    kv_up = jnp.dot(kv_lat_n, wukv_ref[...], preferred_element_type=F32)
    kr = u[:, _C_KR:_C_KR + LANES]
    kr_sw = u[:, _C_KRSW:_C_KRSW + LANES]
    kr_ss = _head_sumsq(kr)
    qk_dim = MLA_NOPE + MLA_ROPE
    for hd in range(HEADS):
        qa = q_up[:, hd * 384:hd * 384 + LANES]
        qb = q_up[:, hd * 384 + LANES:hd * 384 + 2 * LANES]
        qbsw = q_up[:, hd * 384 + 2 * LANES:hd * 384 + 3 * LANES]
        qr = lax.rsqrt((_head_sumsq(qa) + _head_sumsq(qb)) / qk_dim + EPS)
        mq_ref[:, hd * MLA_QK_PAD:hd * MLA_QK_PAD + LANES] = (qa * qr * q_ga).astype(BF16)
        mq_ref[:, hd * MLA_QK_PAD + LANES:(hd + 1) * MLA_QK_PAD] = (
            (qb * q_gb * cos_t + qbsw * q_gbsw * sin_t) * qr).astype(BF16)
        ka = kv_up[:, hd * 256:hd * 256 + LANES]
        kr_inv = lax.rsqrt((_head_sumsq(ka) + kr_ss) / qk_dim + EPS)
        mk_ref[:, hd * MLA_QK_PAD:hd * MLA_QK_PAD + LANES] = (ka * kr_inv * k_ga).astype(BF16)
        mk_ref[:, hd * MLA_QK_PAD + LANES:(hd + 1) * MLA_QK_PAD] = (
            (kr * k_gb * cos_t + kr_sw * k_gbsw * sin_t) * kr_inv).astype(BF16)
        mv_ref[:, hd * HEAD_DIM:(hd + 1) * HEAD_DIM] = kv_up[:, hd * 256 + LANES:(hd + 1) * 256].astype(BF16)


def _proj(x2d, mix_g, w_main, cos_t, sin_t, q_norm, kv_norm, w_uq, w_ukv, gains):
    n, d = x2d.shape
    tm = min(PROJ_ROWS, n)
    row = lambda w: pl.BlockSpec((tm, w), lambda i: (i, 0))
    widths = (CONV_CH, 512, 512, 512, HEADS * MLA_QK_PAD, HEADS * MLA_QK_PAD, 512, 512)
    return pl.pallas_call(
        _proj_kernel,
        grid=(n // tm,),
        in_specs=[row(d), _resident(mix_g.shape), _resident(w_main.shape), row(LANES), row(LANES),
                  _resident(q_norm.shape), _resident(kv_norm.shape), _resident(w_uq.shape),
                  _resident(w_ukv.shape), _resident(gains.shape)],
        out_specs=[row(w) for w in widths],
        out_shape=[jax.ShapeDtypeStruct((n, w), BF16) for w in widths],
        compiler_params=_params("parallel"),
        name="mix_in_proj",
    )(x2d, mix_g, w_main, cos_t, sin_t, q_norm, kv_norm, w_uq, w_ukv, gains)


CONV_ROWS = 256
CONV_CHUNK = 64


def _conv_kernel(y_ref, halo_ref, dw_ref, b_ref, lng_ref, lnb_ref, o_ref, sh_ref):
    j = pl.program_id(1)
    tm = y_ref.shape[1]
    halo = halo_ref[0].astype(F32)
    sh_ref[0, 0:CONV_HALO, :] = jnp.where(j > 0, halo, jnp.zeros_like(halo))
    sh_ref[0, CONV_HALO:, :] = y_ref[0].astype(F32)
    span = tm + CONV_HALO - SUBLANES
    for r in range(1, SUBLANES):
        sh_ref[r, 0:span, :] = sh_ref[0, r:r + span, :]
    dw = dw_ref[...]
    first = CONV_HALO - (CONV_WIDTH - 1)
    for c in range(tm // CONV_CHUNK):
        r0 = c * CONV_CHUNK
        acc = jnp.zeros((CONV_CHUNK, CONV_CH), F32) + b_ref[...]
        for k in range(CONV_WIDTH):
            phase, base = (first + k) % SUBLANES, (first + k) // SUBLANES * SUBLANES
            acc = acc + dw[k:k + 1, :] * sh_ref[phase, r0 + base:r0 + base + CONV_CHUNK, :]
        mu = jnp.mean(acc, axis=-1, keepdims=True)
        cen = acc - mu
        var = jnp.mean(cen * cen, axis=-1, keepdims=True)
        z = cen * lax.rsqrt(var + EPS) * lng_ref[...] + lnb_ref[...]
        o_ref[0, r0:r0 + CONV_CHUNK, :] = (z * jax.nn.sigmoid(z)).astype(BF16)


def _conv_branch(y, dw, b, ln_g, ln_b):
    bsz, s, ch = y.shape
    tm = min(CONV_ROWS, s)
    ratio = tm // CONV_HALO
    return pl.pallas_call(
        _conv_kernel,
        grid=(bsz, s // tm),
        in_specs=[
            pl.BlockSpec((1, tm, ch), lambda b_, j: (b_, j, 0)),
            pl.BlockSpec((1, CONV_HALO, ch), lambda b_, j: (b_, jnp.maximum(j * ratio - 1, 0), 0)),
            _resident(dw.shape), _resident(b.shape), _resident(ln_g.shape), _resident(ln_b.shape),
        ],
        out_specs=pl.BlockSpec((1, tm, ch), lambda b_, j: (b_, j, 0)),
        out_shape=jax.ShapeDtypeStruct((bsz, s, ch), BF16),
        scratch_shapes=[pltpu.VMEM((SUBLANES, tm + CONV_HALO, ch), F32)],
        compiler_params=_params("parallel", "parallel"),
        name="conformer_conv",
    )(y, y, dw, b, ln_g, ln_b)


ATT_BLOCK = 256


def _tn_dot(a, b):
    return lax.dot_general(a, b, (((0,), (0,)), ((), ())), preferred_element_type=F32)


def _sb_blocks(k_blks, v_blks, q_ts, ut, carries, outs_t, strict):
    zs = [jnp.dot(k, q, preferred_element_type=F32) for k, q in zip(k_blks, q_ts)]
    log_keeps = []
    for z in zs:
        log_keep = -(jnp.maximum(z, 0.0) + jnp.log(1.0 + jnp.exp(-jnp.abs(z))))
        if strict is not None:
            log_keep = jnp.where(strict, log_keep, 0.0)
        log_keeps.append(log_keep)
    laters = [jnp.dot(ut, lk.astype(BF16), preferred_element_type=F32) + c for lk, c in zip(log_keeps, carries)]
    new_outs = []
    for z, lk, later, v, out_t in zip(zs, log_keeps, laters, v_blks, outs_t):
        logw = z + lk + later
        if strict is not None:
            logw = jnp.where(strict, logw, MASK_VALUE)
        new_outs.append(out_t + _tn_dot(v, jnp.exp(logw).astype(BF16)))
    new_carries = [c + jnp.sum(lk, axis=0, keepdims=True) for c, lk in zip(carries, log_keeps)]
    return new_carries, new_outs


def _sb_kernel(q_ref, k_ref, v_ref, o_ref, qt_ref, ut_ref, carry_ref, acc_ref):
    i = pl.program_id(1)
    blk = q_ref.shape[1]
    key = lax.broadcasted_iota(jnp.int32, (blk, blk), 0)
    col = lax.broadcasted_iota(jnp.int32, (blk, blk), 1)
    strict = key < col
    ut_ref[...] = (col > key).astype(BF16)
    for hd in range(HEADS):
        qt_ref[hd] = q_ref[0, :, hd * HEAD_DIM:(hd + 1) * HEAD_DIM].astype(F32).T.astype(BF16)
    carry_ref[...] = jnp.zeros(carry_ref.shape, F32)
    acc_ref[...] = jnp.zeros(acc_ref.shape, F32)

    def block(kb, mask):
        start = pl.multiple_of(kb * blk, blk)
        heads = range(HEADS)
        cols = [slice(hd * HEAD_DIM, (hd + 1) * HEAD_DIM) for hd in heads]
        carries, outs_t = _sb_blocks(
            [k_ref[0, pl.ds(start, blk), c] for c in cols], [v_ref[0, pl.ds(start, blk), c] for c in cols],
            [qt_ref[hd] for hd in heads], ut_ref[...], [carry_ref[hd] for hd in heads],
            [acc_ref[hd] for hd in heads], mask)
        top = None
        for hd in heads:
            carry_ref[hd] = carries[hd]
            acc_ref[hd] = outs_t[hd]
            top = carries[hd] if top is None else jnp.maximum(top, carries[hd])
        return jnp.max(top)

    top = block(i, strict)

    def more(state):
        r, top = state
        return jnp.logical_and(r < i, top > SB_UNDERFLOW_LOG)

    def body(state):
        r, _ = state
        return r + 1, block(i - 1 - r, None)

    lax.while_loop(more, body, (jnp.int32(0), top))
    for hd in range(HEADS):
        o_ref[0, :, hd * HEAD_DIM:(hd + 1) * HEAD_DIM] = acc_ref[hd].T.astype(BF16)


def _sb_attention(q, k, v):
    bsz, s, width = q.shape
    blk = min(ATT_BLOCK, s)
    return pl.pallas_call(
        _sb_kernel,
        grid=(bsz, s // blk),
        in_specs=[
            pl.BlockSpec((1, blk, width), lambda b, i: (b, i, 0)),
            pl.BlockSpec((1, s, width), lambda b, i: (b, 0, 0)),
            pl.BlockSpec((1, s, width), lambda b, i: (b, 0, 0)),
        ],
        out_specs=pl.BlockSpec((1, blk, width), lambda b, i: (b, i, 0)),
        out_shape=jax.ShapeDtypeStruct((bsz, s, width), BF16),
        scratch_shapes=[
            pltpu.VMEM((HEADS, HEAD_DIM, blk), BF16),
            pltpu.VMEM((blk, blk), BF16),
            pltpu.VMEM((HEADS, 1, blk), F32),
            pltpu.VMEM((HEADS, HEAD_DIM, blk), F32),
        ],
        compiler_params=_params("parallel", "arbitrary"),
        name="stick_breaking_attention",
    )(q, k, v)


MLA_Q_TILE = 1024
MLA_UPDATE_LANES = 512


def _mla_kernel(q_ref, k_ref, v_ref, o_ref, qt_ref, sa_ref, sb_ref, m_ref, l_ref, acc_ref):
    i = pl.program_id(2)
    tq, tk = q_ref.shape[1], sa_ref.shape[0]
    qt_ref[...] = q_ref[0].astype(F32).T.astype(BF16)
    m_ref[...] = jnp.full(m_ref.shape, MASK_VALUE, F32)
    l_ref[...] = jnp.zeros(l_ref.shape, F32)
    acc_ref[...] = jnp.zeros(acc_ref.shape, F32)
    everyone = slice(0, tq)

    def scores(kb, dst_ref, cols=everyone):
        start = kb * tk if isinstance(kb, int) else pl.multiple_of(kb * tk, tk)
        dst_ref[:, cols] = jnp.dot(k_ref[0, pl.ds(start, tk), :], qt_ref[:, cols], preferred_element_type=F32)

    def update(src_ref, kb, allowed=None, cols=everyone):
        for lo in range(cols.start, cols.stop, MLA_UPDATE_LANES):
            off = lo - cols.start
            mask = None if allowed is None else allowed[:, off:off + MLA_UPDATE_LANES]
            update_piece(src_ref, kb, mask, slice(lo, lo + MLA_UPDATE_LANES))

    def update_piece(src_ref, kb, allowed, cols):
        start = pl.multiple_of(kb * tk, tk)
        s = src_ref[:, cols]
        if allowed is not None:
            s = jnp.where(allowed, s, MASK_VALUE)
        m_prev = m_ref[:, cols]
        m_new = jnp.maximum(m_prev, jnp.max(s, axis=0, keepdims=True))
        alpha = jnp.exp2(m_prev - m_new)
        p = jnp.exp2(s - m_new)
        l_ref[:, cols] = alpha * l_ref[:, cols] + jnp.sum(p, axis=0, keepdims=True)
        acc_ref[:, cols] = alpha * acc_ref[:, cols] + _tn_dot(v_ref[0, pl.ds(start, tk), :], p.astype(BF16))
        m_ref[:, cols] = m_new

    scores(0, sa_ref)

    def pair(p):
        scores(2 * p + 1, sb_ref)
        update(sa_ref, 2 * p)
        scores(2 * p + 2, sa_ref)
        update(sb_ref, 2 * p + 1)

    odd = jnp.bitwise_and(i, 1)
    pl.when(odd == 1)(lambda: pair(0))

    def two_pairs(t, carry):
        pair(odd + 2 * t)
        pair(odd + 2 * t + 1)
        return carry

    lax.fori_loop(0, lax.shift_right_logical(i, 1), two_pairs, 0)

    early, late = slice(0, tk), slice(tk, tq)
    key_chunk = lax.broadcasted_iota(jnp.int32, (tk, tk), 0) // CHUNK
    col_chunk = lax.broadcasted_iota(jnp.int32, (tk, tk), 1) // CHUNK
    block_causal = key_chunk <= col_chunk
    scores(2 * i + 1, sb_ref, late)
    update(sa_ref, 2 * i, block_causal, early)
    update(sa_ref, 2 * i, None, late)
    update(sb_ref, 2 * i + 1, block_causal, late)
    o_ref[0] = (acc_ref[...] / l_ref[...]).T.astype(BF16)


def _mla_attention(q, k, v):
    bsz, s, _ = q.shape
    tq = min(MLA_Q_TILE, s)
    tk = tq // 2
    assert tk % CHUNK == 0 and s % tq == 0
    return pl.pallas_call(
        _mla_kernel,
        grid=(bsz, HEADS, s // tq),
        in_specs=[
            pl.BlockSpec((1, tq, MLA_QK_PAD), lambda b, h, i: (b, i, h)),
            pl.BlockSpec((1, s, MLA_QK_PAD), lambda b, h, i: (b, 0, h)),
            pl.BlockSpec((1, s, HEAD_DIM), lambda b, h, i: (b, 0, h)),
        ],
        out_specs=pl.BlockSpec((1, tq, HEAD_DIM), lambda b, h, i: (b, i, h)),
        out_shape=jax.ShapeDtypeStruct((bsz, s, HEADS * HEAD_DIM), BF16),
        scratch_shapes=[
            pltpu.VMEM((MLA_QK_PAD, tq), BF16),
            pltpu.VMEM((tk, tq), F32),
            pltpu.VMEM((tk, tq), F32),
            pltpu.VMEM((1, tq), F32),
            pltpu.VMEM((1, tq), F32),
            pltpu.VMEM((HEAD_DIM, tq), F32),
        ],
        compiler_params=_params("parallel", "parallel", "arbitrary"),
        name="chunk_causal_latent_attention",
    )(q, k, v)


def _memkv_kernel(mem_ref, g_ref, w_ref, kg_ref, k_ref, v_ref):
    m = mem_ref[0]
    mn = (m * _rms_scale(m) * g_ref[...]).astype(BF16)
    kv = jnp.dot(mn, w_ref[...], preferred_element_type=F32)
    for hd in range(HEADS):
        lo = hd * HEAD_DIM
        k = kv[:, lo:lo + HEAD_DIM]
        k_ref[0, :, lo:lo + HEAD_DIM] = (k * lax.rsqrt(_head_sumsq(k) / HEAD_DIM + EPS) * kg_ref[...]).astype(BF16)
    v_ref[0] = kv[:, HEADS * HEAD_DIM:].astype(BF16)


def _memkv(mem, g, w_kv, k_gain):
    bsz, m, d = mem.shape
    width = HEADS * HEAD_DIM
    blk = lambda w: pl.BlockSpec((1, m, w), lambda b: (b, 0, 0))
    return pl.pallas_call(
        _memkv_kernel,
        grid=(bsz,),
        in_specs=[blk(d), _resident(g.shape), _resident(w_kv.shape), _resident(k_gain.shape)],
        out_specs=[blk(width), blk(width)],
        out_shape=[jax.ShapeDtypeStruct((bsz, m, width), BF16)] * 2,
        compiler_params=_params("parallel"),
        name="memory_kv",
    )(mem, g, w_kv, k_gain)


MERGE_ROWS = 512


def _merge_kernel(x_ref, mixg_ref, wg_ref, wb_ref, wo_ref, conv_ref, sb_ref, mla_ref, memq_ref, memk_ref, memv_ref,
                  o_ref, memo_ref):
    x = x_ref[0]
    h = (x * _rms_scale(x) * mixg_ref[...]).astype(BF16)

    cols = [slice(hd * HEAD_DIM, (hd + 1) * HEAD_DIM) for hd in range(HEADS)]
    ss = [lax.dot_general(memq_ref[0, :, c], memk_ref[0, :, c], (((1,), (1,)), ((), ())),
                          preferred_element_type=F32) for c in cols]
    ps = [jnp.exp(s - jnp.max(s, axis=-1, keepdims=True)) for s in ss]
    outs = [jnp.dot(p.astype(BF16), memv_ref[0, :, c], preferred_element_type=F32) for p, c in zip(ps, cols)]
    for p, o, c in zip(ps, outs, cols):
        memo_ref[:, c] = (o / jnp.sum(p, axis=-1, keepdims=True)).astype(BF16)

    branches = (conv_ref[0], sb_ref[0], mla_ref[0], memo_ref[...])
    merged = None
    for i, br in enumerate(branches):
        gate = jax.nn.sigmoid(jnp.dot(h, wg_ref[:, i * D_MODEL:(i + 1) * D_MODEL], preferred_element_type=F32))
        term = gate * jnp.dot(br, wb_ref[i], preferred_element_type=F32)
        merged = term if merged is None else merged + term
    o_ref[0] = x + jnp.dot(merged.astype(BF16), wo_ref[...], preferred_element_type=F32)


def _merge(x, mix_g, w_gate, w_branch, w_out, conv_o, sb_o, mla_o, mem_q, mem_k, mem_v):
    bsz, s, d = x.shape
    tm = min(MERGE_ROWS, s)
    row = lambda w: pl.BlockSpec((1, tm, w), lambda b, j: (b, j, 0))
    mem = pl.BlockSpec((1,) + mem_k.shape[1:], lambda b, j: (b, 0, 0))
    return pl.pallas_call(
        _merge_kernel,
        grid=(bsz, s // tm),
        in_specs=[row(d), _resident(mix_g.shape), _resident(w_gate.shape), _resident(w_branch.shape),
                  _resident(w_out.shape), row(BRANCH_WIDTH), row(BRANCH_WIDTH), row(BRANCH_WIDTH),
                  row(BRANCH_WIDTH), mem, mem],
        out_specs=row(d),
        out_shape=jax.ShapeDtypeStruct((bsz, s, d), F32),
        scratch_shapes=[pltpu.VMEM((tm, BRANCH_WIDTH), BF16)],
        compiler_params=_params("parallel", "parallel"),
        name="gated_merge_out_proj",
    )(x, mix_g, w_gate, w_branch, w_out, conv_o, sb_o, mla_o, mem_q, mem_k, mem_v)


def _pad_lanes(a, width=LANES):
    return jnp.pad(a, [(0, 0)] * (a.ndim - 1) + [(0, width - a.shape[-1])])


def _swap_halves(a):
    half = a.shape[-1] // 2
    return jnp.concatenate([a[..., half:], a[..., :half]], axis=-1)


def _rope_tables(positions):
    inv_freq = ROPE_BASE ** (-jnp.arange(0, MLA_ROPE, 2, dtype=F32) / MLA_ROPE)
    ang = positions.astype(F32)[..., None] * inv_freq
    cos, sin = jnp.cos(ang), jnp.sin(ang)
    cos_t = _pad_lanes(jnp.concatenate([cos, cos], axis=-1))
    sin_t = _pad_lanes(jnp.concatenate([-sin, sin], axis=-1))
    return cos_t, sin_t


def _layer_params(l, w_in, sb_q_hnorm, sb_k_hnorm, mla_w_uq, mla_q_hnorm, mla_k_hnorm, mem_q_hnorm):
    w = w_in[l]
    kr = w[:, 3072:3136]
    w_main = jnp.concatenate(
        [w[:, :3072], _pad_lanes(kr), _pad_lanes(_swap_halves(kr)), w[:, 3136:3648]], axis=1).astype(BF16)
    w_gate = w[:, 3648:].astype(BF16)

    wuq = mla_w_uq[l].reshape(MLA_Q_LORA, HEADS, MLA_NOPE + MLA_ROPE)
    rot = wuq[..., MLA_NOPE:]
    w_uq = jnp.concatenate([wuq[..., :MLA_NOPE], _pad_lanes(rot), _pad_lanes(_swap_halves(rot))], axis=-1)
    w_uq = w_uq.reshape(MLA_Q_LORA, HEADS * 3 * LANES).astype(BF16)

    sb_scale = HEAD_DIM ** -0.5
    mla_scale = (MLA_NOPE + MLA_ROPE) ** -0.5 * LOG2_E
    qg, kg = mla_q_hnorm[l] * mla_scale, mla_k_hnorm[l]
    rows = [sb_q_hnorm[l] * sb_scale, sb_k_hnorm[l], mem_q_hnorm[l] * sb_scale,
            qg[:MLA_NOPE], _pad_lanes(qg[MLA_NOPE:]), _pad_lanes(_swap_halves(qg[MLA_NOPE:])),
            kg[:MLA_NOPE], _pad_lanes(kg[MLA_NOPE:]), _pad_lanes(_swap_halves(kg[MLA_NOPE:]))]
    gains = jnp.stack(rows + [jnp.zeros((LANES,), F32)] * (16 - len(rows)))
    return w_main, w_gate, w_uq, gains


def kernel(x, mem, positions, ffn1_norm, ffn1_w_in, ffn1_w_out, mix_norm, w_in, conv_dw, conv_b, conv_ln_g,
           conv_ln_b, sb_q_hnorm, sb_k_hnorm, mla_q_norm, mla_w_uq, mla_kv_norm, mla_w_ukv, mla_q_hnorm,
           mla_k_hnorm, mem_norm, mem_w_kv, mem_q_hnorm, mem_k_hnorm, w_branch, w_out, ffn2_norm, ffn2_w_in,
           ffn2_w_out):
    bsz, s, d = x.shape
    depth = w_in.shape[0]
    n = bsz * s
    cos_t, sin_t = _rope_tables(positions)
    cos_t, sin_t = cos_t.reshape(n, LANES), sin_t.reshape(n, LANES)
    row = lambda a: a.reshape(1, -1)

    for l in range(depth):
        w_main, w_gate, w_uq, gains = _layer_params(
            l, w_in, sb_q_hnorm, sb_k_hnorm, mla_w_uq, mla_q_hnorm, mla_k_hnorm, mem_q_hnorm)

        x2d = _ffn(x.reshape(n, d), row(ffn1_norm[l]), ffn1_w_in[l].astype(BF16), ffn1_w_out[l].astype(BF16))

        y, sbq, sbk, sbv, mq, mk, mv, memq = _proj(
            x2d, row(mix_norm[l]), w_main, cos_t, sin_t, row(mla_q_norm[l]), row(mla_kv_norm[l]), w_uq,
            mla_w_ukv[l].astype(BF16), gains)
        seq = lambda a: a.reshape(bsz, s, a.shape[-1])

        conv_o = _conv_branch(seq(y), conv_dw[l], row(conv_b[l]), row(conv_ln_g[l]), row(conv_ln_b[l]))
        sb_o = _sb_attention(seq(sbq), seq(sbk), seq(sbv))
        mla_o = _mla_attention(seq(mq), seq(mk), seq(mv))
        mem_k, mem_v = _memkv(mem, row(mem_norm[l]), mem_w_kv[l].astype(BF16), row(mem_k_hnorm[l]))

        x = _merge(seq(x2d), row(mix_norm[l]), w_gate, w_branch[l].astype(BF16), w_out[l].astype(BF16),
                   conv_o, sb_o, mla_o, seq(memq), mem_k, mem_v)

        x = _ffn(x.reshape(n, d), row(ffn2_norm[l]), ffn2_w_in[l].astype(BF16),
                 ffn2_w_out[l].astype(BF16)).reshape(bsz, s, d)
    return x
```
